```python
import math
import jax, jax.numpy as jnp
from jax import lax
import numpy as np

D_MODEL = 1024
BATCH = 8
SEQ = 8192
DEPTH = 1

CHUNK = 64
Q_BLOCK = 128
POOL_WINDOWS = (2, 4, 8, 16)
POOL_WIDTH = D_MODEL // 2
POOL_GROUP = POOL_WIDTH // len(POOL_WINDOWS)
N_HEADS = D_MODEL // 128
HEAD_DIM = 64
V_DIM = 2 * HEAD_DIM
ATTN_QK_WIDTH = N_HEADS * 2 * HEAD_DIM
ATTN_V_WIDTH = N_HEADS * V_DIM
N_BRANCH = 2
IN_WIDTH = POOL_WIDTH + 2 * ATTN_QK_WIDTH + ATTN_V_WIDTH + N_BRANCH * D_MODEL
D_FF = 4 * D_MODEL
ROPE_THETA = 500000.0
ROPE_DIM = HEAD_DIM // 4
NORM_EPS = 1e-6
SUBLN_EPS = 1e-5

kernel_name = "hybrid_pool_diffattn_gated_block"


def rms_norm(x, g, eps=NORM_EPS):
    x32 = x.astype(jnp.float32)
    y = x32 * lax.rsqrt(jnp.mean(x32 * x32, axis=-1, keepdims=True) + eps)
    return (y * g.astype(jnp.float32)).astype(x.dtype)


def rope_tables(seq):
    pos = jnp.arange(seq, dtype=jnp.float32)
    inv = ROPE_THETA ** (-jnp.arange(0, ROPE_DIM, 2, dtype=jnp.float32) / ROPE_DIM)
    ang = pos[:, None] * inv[None, :]
    return jnp.cos(ang), jnp.sin(ang)


def partial_rope(t, cos, sin):
    half = ROPE_DIM // 2
    c = cos[None, :, None, None, :].astype(t.dtype)
    s = sin[None, :, None, None, :].astype(t.dtype)
    t1 = t[..., :half]
    t2 = t[..., half:ROPE_DIM]
    return jnp.concatenate([t1 * c - t2 * s, t2 * c + t1 * s, t[..., ROPE_DIM:]], axis=-1)


def multiscale_pool(u, w_group, scale):
    B, S, _ = u.shape
    ug = u.reshape(B, S, len(POOL_WINDOWS), POOL_GROUP)
    t = jnp.arange(1, S + 1, dtype=jnp.float32)
    outs = []
    for gi, w in enumerate(POOL_WINDOWS):
        xg = ug[:, :, gi, :].astype(jnp.float32)
        cs = jnp.cumsum(xg, axis=1)
        cs_prev = jnp.pad(cs, ((0, 0), (w, 0), (0, 0)))[:, :S]
        count = jnp.minimum(t, float(w))
        mean = (cs - cs_prev) / count[None, :, None]
        outs.append((mean - xg).astype(u.dtype))
    pooled = jnp.stack(outs, axis=2)
    mixed = jnp.einsum('bsgp,gpq->bsgq', pooled, w_group)
    return mixed.reshape(B, S, POOL_WIDTH) * scale


def diff_attention(q, k, v, lam):
    B, S = q.shape[0], q.shape[1]
    nb = S // Q_BLOCK
    qb = q.reshape(B, nb, Q_BLOCK, N_HEADS, 2, HEAD_DIM).transpose(1, 0, 2, 3, 4, 5)
    key_chunk = jnp.arange(S) // CHUNK
    scale = HEAD_DIM ** -0.5

    def one_block(args):
        i, qi = args
        q_chunk = (i * Q_BLOCK + jnp.arange(Q_BLOCK)) // CHUNK
        allowed = key_chunk[None, :] <= q_chunk[:, None]
        s = jnp.einsum('bqhcd,bkhcd->bhcqk', qi, k,
                       preferred_element_type=jnp.float32) * scale
        s = jnp.where(allowed, s, -jnp.inf)
        p = jax.nn.softmax(s, axis=-1)
        a = p[:, :, 0] - lam * p[:, :, 1]
        return jnp.einsum('bhqk,bkhe->bqhe', a.astype(v.dtype), v)

    out = lax.map(one_block, (jnp.arange(nb), qb))
    return out.transpose(1, 0, 2, 3, 4).reshape(B, S, N_HEADS, V_DIM)


def setup_inputs(seed: int = 0) -> dict:
    key = jax.random.key(seed)
    ks = jax.random.split(key, 20)
    f32 = jnp.float32
    nrm = lambda k, shp, s: jax.random.normal(k, shp, f32) * s
    L = DEPTH
    return {
        "x": nrm(ks[0], (BATCH, SEQ, D_MODEL), 1.0),
        "w_in": nrm(ks[1], (L, D_MODEL, IN_WIDTH), D_MODEL ** -0.5),
        "b_gate": nrm(ks[2], (L, N_BRANCH, D_MODEL), 0.1),
        "pool_w": nrm(ks[3], (L, len(POOL_WINDOWS), POOL_GROUP, POOL_GROUP), POOL_GROUP ** -0.5),
        "pool_scale": 1.0 + nrm(ks[4], (L, POOL_WIDTH), 0.1),
        "lambda_q1": nrm(ks[5], (L, HEAD_DIM), 0.1),
        "lambda_k1": nrm(ks[6], (L, HEAD_DIM), 0.1),
        "lambda_q2": nrm(ks[7], (L, HEAD_DIM), 0.1),
        "lambda_k2": nrm(ks[8], (L, HEAD_DIM), 0.1),
        "g_subln": 1.0 + nrm(ks[9], (L, V_DIM), 0.1),
        "w_pool_out": nrm(ks[10], (L, POOL_WIDTH, D_MODEL), POOL_WIDTH ** -0.5),
        "w_attn_out": nrm(ks[11], (L, ATTN_V_WIDTH, D_MODEL), ATTN_V_WIDTH ** -0.5),
        "w_o": nrm(ks[12], (L, D_MODEL, D_MODEL), D_MODEL ** -0.5),
        "g_mix": 1.0 + nrm(ks[13], (L, D_MODEL), 0.1),
        "g_mlp": 1.0 + nrm(ks[14], (L, D_MODEL), 0.1),
        "w_up": nrm(ks[15], (L, D_MODEL, D_FF), D_MODEL ** -0.5),
        "w_down": nrm(ks[16], (L, D_FF, D_MODEL), D_FF ** -0.5),
        "g_final": 1.0 + nrm(ks[17], (D_MODEL,), 0.1),
    }


def reference(x, w_in, b_gate, pool_w, pool_scale, lambda_q1, lambda_k1, lambda_q2, lambda_k2,
              g_subln, w_pool_out, w_attn_out, w_o, g_mix, g_mlp, w_up, w_down, g_final):
    B, S, _ = x.shape
    cos, sin = rope_tables(S)
    o1 = POOL_WIDTH
    o2 = o1 + ATTN_QK_WIDTH
    o3 = o2 + ATTN_QK_WIDTH
    o4 = o3 + ATTN_V_WIDTH
    for l in range(DEPTH):
        lambda_init = 0.8 - 0.6 * math.exp(-0.3 * l)
        h = rms_norm(x, g_mix[l])
        proj = h @ w_in[l]
        u_pool = proj[..., :o1]
        q = proj[..., o1:o2].reshape(B, S, N_HEADS, 2, HEAD_DIM)
        k = proj[..., o2:o3].reshape(B, S, N_HEADS, 2, HEAD_DIM)
        v = proj[..., o3:o4].reshape(B, S, N_HEADS, V_DIM)
        gates = jax.nn.sigmoid(proj[..., o4:].reshape(B, S, N_BRANCH, D_MODEL) + b_gate[l])
        y_pool = multiscale_pool(u_pool, pool_w[l], pool_scale[l]) @ w_pool_out[l]
        q = partial_rope(q, cos, sin)
        k = partial_rope(k, cos, sin)
        lam = (jnp.exp(jnp.sum(lambda_q1[l].astype(jnp.float32) * lambda_k1[l].astype(jnp.float32)))
               - jnp.exp(jnp.sum(lambda_q2[l].astype(jnp.float32) * lambda_k2[l].astype(jnp.float32)))
               + lambda_init)
        att = diff_attention(q, k, v, lam)
        att = rms_norm(att, g_subln[l], SUBLN_EPS) * (1.0 - lambda_init)
        y_attn = att.reshape(B, S, ATTN_V_WIDTH) @ w_attn_out[l]
        merged = gates[:, :, 0] * y_pool + gates[:, :, 1] * y_attn
        x = x + merged @ w_o[l]
        h2 = rms_norm(x, g_mlp[l])
        x = x + jnp.square(jax.nn.relu(h2 @ w_up[l])) @ w_down[l]
    return rms_norm(x, g_final)
```

```python
import functools
import math

import jax
import jax.numpy as jnp
from jax import lax
from jax.experimental import pallas as pl
from jax.experimental.pallas import tpu as pltpu

CHUNK = 64
POOL_WINDOWS = (2, 4, 8, 16)
HEAD_DIM = 64
V_DIM = 2 * HEAD_DIM
ROPE_THETA = 500000.0
ROPE_DIM = HEAD_DIM // 4
ROPE_HALF = ROPE_DIM // 2
NORM_EPS = 1e-6
SUBLN_EPS = 1e-5
LOG2E = math.log2(math.e)

LANES = 128
SUBLANES = 8
VMEM_LIMIT_BYTES = 56 * 1024 * 1024

PROJ_TOKENS = 512
MLP_TOKENS = 512
FF_CHUNK = 1024
ATT_Q = 256
ATT_K = 256
POOL_HALO = 16

BF16 = jnp.bfloat16
F32 = jnp.float32


def _const_spec(shape):
    zeros = (0,) * len(shape)
    return pl.BlockSpec(shape, lambda *_: zeros, pipeline_mode=pl.Buffered(1))


def _rms(x, g, eps):
    ms = jnp.mean(x * x, axis=-1, keepdims=True)
    return x * lax.rsqrt(ms + eps) * g


def _in_proj_kernel(x_ref, g_ref, wu_ref, wk_ref, wg_ref, wqt_ref, wvt_ref, bg_ref,
                    pw_ref, ps_ref, wpo_ref, kc_ref, ksa_ref, ksb_ref, qc_ref, qs_ref,
                    k_out, qt_out, vt_out, gy_out, g1_out, ext_ref, *, q_scale):
    tm = x_ref.shape[0]
    d_model = x_ref.shape[1]
    j = pl.program_id(1)

    h = _rms(x_ref[...], g_ref[...], NORM_EPS).astype(BF16)

    k = jnp.dot(h, wk_ref[...], preferred_element_type=F32)
    kc, ksa, ksb = kc_ref[...], ksa_ref[...], ksb_ref[...]
    for hb in range(k.shape[1] // LANES):
        kh = k[:, hb * LANES:(hb + 1) * LANES]
        rot = (kh * kc
               + pltpu.roll(kh, LANES - ROPE_HALF, 1) * ksa
               + pltpu.roll(kh, ROPE_HALF, 1) * ksb)
        k_out[:, hb * LANES:(hb + 1) * LANES] = rot.astype(BF16)

    nt = (((1,), (1,)), ((), ()))
    qt = lax.dot_general(wqt_ref[...], h, nt, preferred_element_type=F32)
    qc, qs = qc_ref[...], qs_ref[...]
    for g in range(qt.shape[0] // HEAD_DIM):
        base = g * HEAD_DIM
        t1 = qt[base:base + ROPE_HALF]
        t2 = qt[base + ROPE_HALF:base + ROPE_DIM]
        rest = qt[base + ROPE_DIM:base + HEAD_DIM]
        grp = jnp.concatenate([t1 * qc - t2 * qs, t2 * qc + t1 * qs, rest], axis=0)
        qt_out[base:base + HEAD_DIM, :] = (grp * q_scale).astype(BF16)
    vt = lax.dot_general(wvt_ref[...], h, nt, preferred_element_type=F32)
    vt_out[...] = vt.astype(BF16)

    u = jnp.dot(h, wu_ref[...], preferred_element_type=F32)

    @pl.when(j == 0)
    def _():
        ext_ref[0:POOL_HALO, :] = jnp.zeros((POOL_HALO, ext_ref.shape[1]), F32)

    ext_ref[POOL_HALO:POOL_HALO + tm, :] = u
    pos = (j * tm + lax.broadcasted_iota(jnp.int32, (tm, 1), 0) + 1).astype(F32)
    pgw = pw_ref.shape[1]
    mixed = []
    for gi, w in enumerate(POOL_WINDOWS):
        cols = slice(gi * pgw, (gi + 1) * pgw)
        acc = u[:, cols]
        for s in range(1, w):
            acc = acc + ext_ref[POOL_HALO - s:POOL_HALO - s + tm, cols]
        mean = acc / jnp.minimum(pos, float(w))
        pooled = (mean - u[:, cols]).astype(BF16)
        mixed.append(jnp.dot(pooled, pw_ref[gi], preferred_element_type=F32))
    mixed = jnp.concatenate(mixed, axis=1) * ps_ref[...]
    y_pool = jnp.dot(mixed.astype(BF16), wpo_ref[...], preferred_element_type=F32)
    ext_ref[0:POOL_HALO, :] = ext_ref[tm:tm + POOL_HALO, :]

    gl = jnp.dot(h, wg_ref[...], preferred_element_type=F32)
    bg = bg_ref[...]
    g0 = jax.nn.sigmoid(gl[:, :d_model] + bg[0:1, :])
    g1 = jax.nn.sigmoid(gl[:, d_model:] + bg[1:2, :])
    gy_out[...] = (g0 * y_pool).astype(BF16)
    g1_out[...] = g1.astype(BF16)


def _in_proj(x, g_mix, wu, wk, wg, wqt, wvt, b_gate, pool_w, pool_scale, wpo,
             kc, ksa, ksb, qc, qs, q_scale):
    B, S, D = x.shape
    tm = PROJ_TOKENS
    nj = S // tm
    tok = lambda b, j: (b, j, 0)
    feat = lambda b, j: (b, 0, j)
    out_shapes = (
        jax.ShapeDtypeStruct((B, S, wk.shape[1]), BF16),
        jax.ShapeDtypeStruct((B, wqt.shape[0], S), BF16),
        jax.ShapeDtypeStruct((B, wvt.shape[0], S), BF16),
        jax.ShapeDtypeStruct((B, S, D), BF16),
        jax.ShapeDtypeStruct((B, S, D), BF16),
    )
    return pl.pallas_call(
        functools.partial(_in_proj_kernel, q_scale=q_scale),
        grid=(B, nj),
        in_specs=[
            pl.BlockSpec((None, tm, D), tok),
            _const_spec(g_mix.shape),
            _const_spec(wu.shape), _const_spec(wk.shape), _const_spec(wg.shape),
            _const_spec(wqt.shape), _const_spec(wvt.shape), _const_spec(b_gate.shape),
            _const_spec(pool_w.shape), _const_spec(pool_scale.shape), _const_spec(wpo.shape),
            pl.BlockSpec((tm, LANES), lambda b, j: (j, 0)),
            pl.BlockSpec((tm, LANES), lambda b, j: (j, 0)),
            pl.BlockSpec((tm, LANES), lambda b, j: (j, 0)),
            pl.BlockSpec((ROPE_HALF, tm), lambda b, j: (0, j)),
            pl.BlockSpec((ROPE_HALF, tm), lambda b, j: (0, j)),
        ],
        out_specs=(
            pl.BlockSpec((None, tm, wk.shape[1]), tok),
            pl.BlockSpec((None, wqt.shape[0], tm), feat),
            pl.BlockSpec((None, wvt.shape[0], tm), feat),
            pl.BlockSpec((None, tm, D), tok),
            pl.BlockSpec((None, tm, D), tok),
        ),
        out_shape=out_shapes,
        scratch_shapes=[pltpu.VMEM((tm + POOL_HALO, wu.shape[1]), F32)],
        compiler_params=pltpu.CompilerParams(
            dimension_semantics=("arbitrary", "arbitrary"),
            vmem_limit_bytes=VMEM_LIMIT_BYTES),
        name="in_proj",
    )(x, g_mix, wu, wk, wg, wqt, wvt, b_gate, pool_w, pool_scale, wpo, kc, ksa, ksb, qc, qs)


def _diff_attn_kernel(lam_ref, k_ref, qt_ref, vt_ref, gs_ref, o_ref,
                      m_ref, l_ref, acc_ref, *, lambda_init):
    S = k_ref.shape[0]
    tq, tk = ATT_Q, ATT_K
    nq = S // tq

    lp = lam_ref[...]
    lam = (jnp.exp(jnp.sum(lp[0:1] * lp[1:2], axis=1, keepdims=True))
           - jnp.exp(jnp.sum(lp[2:3] * lp[3:4], axis=1, keepdims=True))
           + lambda_init)

    row = lax.broadcasted_iota(jnp.int32, (V_DIM, tq), 0)
    key_chunk = lax.broadcasted_iota(jnp.int32, (tk, 2 * tq), 0) // CHUNK
    col = lax.broadcasted_iota(jnp.int32, (tk, 2 * tq), 1)
    qry_chunk = jnp.where(col >= tq, col - tq, col) // CHUNK
    diag_allowed = key_chunk <= qry_chunk

    def q_tile(i, carry):
        q0 = pl.multiple_of(i * tq, tq)
        qt = qt_ref[:, pl.ds(q0, tq)]
        zero = jnp.zeros_like(qt)
        wq = jnp.concatenate([jnp.where(row < HEAD_DIM, qt, zero),
                              jnp.where(row >= HEAD_DIM, qt, zero)], axis=1)
        m_ref[...] = jnp.full(m_ref.shape, -jnp.inf, F32)
        l_ref[...] = jnp.zeros(l_ref.shape, F32)
        acc_ref[...] = jnp.zeros(acc_ref.shape, F32)

        def kv_tile(jk, masked):
            k0 = pl.multiple_of(jk * tk, tk)
            s = jnp.dot(k_ref[pl.ds(k0, tk), :], wq, preferred_element_type=F32)
            if masked:
                s = jnp.where(diag_allowed, s, -jnp.inf)
            m_old = m_ref[...]
            m_new = jnp.maximum(m_old, jnp.max(s, axis=0, keepdims=True))
            alpha = jnp.exp2(m_old - m_new)
            p = jnp.exp2(s - m_new)
            l_ref[...] = alpha * l_ref[...] + jnp.sum(p, axis=0, keepdims=True)
            pv = jnp.dot(vt_ref[:, pl.ds(k0, tk)], p.astype(BF16),
                         preferred_element_type=F32)
            acc_ref[...] = alpha * acc_ref[...] + pv
            m_ref[...] = m_new

        def kv_body(jk, c):
            kv_tile(jk, False)
            return c

        lax.fori_loop(0, i, kv_body, 0)
        kv_tile(i, True)

        o = acc_ref[...] / l_ref[...]
        a = o[:, :tq] - lam * o[:, tq:]
        ms = jnp.mean(a * a, axis=0, keepdims=True)
        a = a * lax.rsqrt(ms + SUBLN_EPS) * gs_ref[...] * (1.0 - lambda_init)
        o_ref[pl.ds(q0, tq), :] = a.T.astype(BF16)
        return carry

    lax.fori_loop(0, nq, q_tile, 0)


def _diff_attn(lam_params, k, qt, vt, g_subln_col, lambda_init):
    B, S, HW = k.shape
    H = HW // LANES
    return pl.pallas_call(
        functools.partial(_diff_attn_kernel, lambda_init=lambda_init),
        grid=(B, H),
        in_specs=[
            _const_spec(lam_params.shape),
            pl.BlockSpec((None, S, LANES), lambda b, h: (b, 0, h)),
            pl.BlockSpec((None, LANES, S), lambda b, h: (b, h, 0)),
            pl.BlockSpec((None, V_DIM, S), lambda b, h: (b, h, 0)),
            _const_spec(g_subln_col.shape),
        ],
        out_specs=pl.BlockSpec((None, S, V_DIM), lambda b, h: (b, 0, h)),
        out_shape=jax.ShapeDtypeStruct((B, S, H * V_DIM), BF16),
        scratch_shapes=[
            pltpu.VMEM((1, 2 * ATT_Q), F32),
            pltpu.VMEM((1, 2 * ATT_Q), F32),
            pltpu.VMEM((V_DIM, 2 * ATT_Q), F32),
        ],
        compiler_params=pltpu.CompilerParams(
            dimension_semantics=("arbitrary", "arbitrary"),
            vmem_limit_bytes=VMEM_LIMIT_BYTES),
        name="diff_attn",
    )(lam_params, k, qt, vt, g_subln_col)


def _out_mlp_kernel(x_ref, att_ref, gy_ref, g1_ref, wao_ref, wo_ref, gm_ref, wup_ref,
                    wdn_ref, gn_ref, o_ref, *, final_norm):
    y_attn = jnp.dot(att_ref[...], wao_ref[...], preferred_element_type=F32)
    merged = gy_ref[...].astype(F32) + g1_ref[...].astype(F32) * y_attn
    x1 = x_ref[...] + jnp.dot(merged.astype(BF16), wo_ref[...], preferred_element_type=F32)
    h2 = _rms(x1, gm_ref[...], NORM_EPS).astype(BF16)
    acc = x1
    d_ff = wup_ref.shape[1]
    for c in range(d_ff // FF_CHUNK):
        cols = slice(c * FF_CHUNK, (c + 1) * FF_CHUNK)
        up = jnp.dot(h2, wup_ref[:, cols], preferred_element_type=F32)
        act = jnp.square(jnp.maximum(up, 0.0)).astype(BF16)
        acc = acc + jnp.dot(act, wdn_ref[cols, :], preferred_element_type=F32)
    o_ref[...] = _rms(acc, gn_ref[...], NORM_EPS) if final_norm else acc


def _out_mlp(x, att, gy, g1, wao, wo, g_mlp, wup, wdn, g_final, final_norm):
    B, S, D = x.shape
    tm = MLP_TOKENS
    tok = lambda b, j: (b, j, 0)
    tile = pl.BlockSpec((None, tm, D), tok)
    return pl.pallas_call(
        functools.partial(_out_mlp_kernel, final_norm=final_norm),
        grid=(B, S // tm),
        in_specs=[tile, tile, tile, tile,
                  _const_spec(wao.shape), _const_spec(wo.shape), _const_spec(g_mlp.shape),
                  _const_spec(wup.shape), _const_spec(wdn.shape), _const_spec(g_final.shape)],
        out_specs=tile,
        out_shape=jax.ShapeDtypeStruct((B, S, D), F32),
        compiler_params=pltpu.CompilerParams(
            dimension_semantics=("arbitrary", "arbitrary"),
            vmem_limit_bytes=VMEM_LIMIT_BYTES),
        name="out_mlp",
    )(x, att, gy, g1, wao, wo, g_mlp, wup, wdn, g_final)


def _rope_tables(seq):
    pos = jnp.arange(seq, dtype=F32)
    inv = ROPE_THETA ** (-jnp.arange(0, ROPE_DIM, 2, dtype=F32) / ROPE_DIM)
    ang = pos[:, None] * inv[None, :]
    cos, sin = jnp.cos(ang), jnp.sin(ang)
    ones = jnp.ones((seq, HEAD_DIM - ROPE_DIM), F32)
    zeros = jnp.zeros((seq, HEAD_DIM - ROPE_HALF), F32)
    zeros_h = jnp.zeros((seq, ROPE_HALF), F32)
    c64 = jnp.concatenate([cos, cos, ones], axis=1)
    sa64 = jnp.concatenate([-sin, zeros], axis=1)
    sb64 = jnp.concatenate([zeros_h, sin, zeros[:, :HEAD_DIM - ROPE_DIM]], axis=1)
    tile2 = lambda t: jnp.concatenate([t, t], axis=1)
    return tile2(c64), tile2(sa64), tile2(sb64), cos.T, sin.T


def kernel(x, w_in, b_gate, pool_w, pool_scale, lambda_q1, lambda_k1, lambda_q2, lambda_k2,
           g_subln, w_pool_out, w_attn_out, w_o, g_mix, g_mlp, w_up, w_down, g_final):
    B, S, D = x.shape
    depth = w_in.shape[0]
    pool_width = pool_scale.shape[1]
    qk_width = (w_in.shape[2] - pool_width - 2 * D) // 3
    o1 = pool_width
    o2 = o1 + qk_width
    o3 = o2 + qk_width
    o4 = o3 + qk_width
    kc, ksa, ksb, qc, qs = _rope_tables(S)
    q_scale = HEAD_DIM ** -0.5 * LOG2E

    for l in range(depth):
        lambda_init = 0.8 - 0.6 * math.exp(-0.3 * l)
        w = w_in[l]
        wu = w[:, :o1].astype(BF16)
        wqt = w[:, o1:o2].T.astype(BF16)
        wk = w[:, o2:o3].astype(BF16)
        wvt = w[:, o3:o4].T.astype(BF16)
        wg = w[:, o4:].astype(BF16)
        k, qt, vt, gy, g1 = _in_proj(
            x, g_mix[l][None, :], wu, wk, wg, wqt, wvt, b_gate[l],
            pool_w[l].astype(BF16), pool_scale[l][None, :], w_pool_out[l].astype(BF16),
            kc, ksa, ksb, qc, qs, q_scale)
        lam_params = jnp.stack([lambda_q1[l], lambda_k1[l], lambda_q2[l], lambda_k2[l]])
        att = _diff_attn(lam_params.astype(F32), k, qt, vt,
                         g_subln[l].astype(F32)[:, None], lambda_init)
        x = _out_mlp(x, att, gy, g1, w_attn_out[l].astype(BF16), w_o[l].astype(BF16),
                     g_mlp[l][None, :], w_up[l].astype(BF16), w_down[l].astype(BF16),
                     g_final[None, :], final_norm=(l == depth - 1))
    return x
```

```python
import functools
import math

import jax
import jax.numpy as jnp
from jax import lax
from jax.experimental import pallas as pl
from jax.experimental.pallas import tpu as pltpu

CHUNK = 64
POOL_WINDOWS = (2, 4, 8, 16)
HEAD_DIM = 64
V_DIM = 2 * HEAD_DIM
ROPE_THETA = 500000.0
ROPE_DIM = HEAD_DIM // 4
ROPE_HALF = ROPE_DIM // 2
NORM_EPS = 1e-6
SUBLN_EPS = 1e-5
LOG2E = math.log2(math.e)

LANES = 128
SUBLANES = 8
VMEM_LIMIT_BYTES = 56 * 1024 * 1024

PROJ_TOKENS = 512
MLP_TOKENS = 512
FF_CHUNK = 1024
ATT_Q = 1024
ATT_K = 256
assert ATT_Q % (2 * ATT_K) == 0 and ATT_K % CHUNK == 0
POOL_HALO = 16

BF16 = jnp.bfloat16
F32 = jnp.float32


def _const_spec(shape):
    zeros = (0,) * len(shape)
    return pl.BlockSpec(shape, lambda *_: zeros, pipeline_mode=pl.Buffered(1))


def _rms(x, g, eps):
    ms = jnp.mean(x * x, axis=-1, keepdims=True)
    return x * lax.rsqrt(ms + eps) * g


def _in_proj_kernel(x_ref, g_ref, wu_ref, wk_ref, wg_ref, wqt_ref, wvt_ref, bg_ref,
                    pw_ref, ps_ref, wpo_ref, kc_ref, ksa_ref, ksb_ref, qc_ref, qs_ref,
                    k_out, qt_out, vt_out, gy_out, g1_out, ext_ref, *, q_scale):
    tm = x_ref.shape[0]
    d_model = x_ref.shape[1]
    j = pl.program_id(1)

    h = _rms(x_ref[...], g_ref[...], NORM_EPS).astype(BF16)

    k = jnp.dot(h, wk_ref[...], preferred_element_type=F32)
    kc, ksa, ksb = kc_ref[...], ksa_ref[...], ksb_ref[...]
    for hb in range(k.shape[1] // LANES):
        kh = k[:, hb * LANES:(hb + 1) * LANES]
        rot = (kh * kc
               + pltpu.roll(kh, LANES - ROPE_HALF, 1) * ksa
               + pltpu.roll(kh, ROPE_HALF, 1) * ksb)
        k_out[:, hb * LANES:(hb + 1) * LANES] = rot.astype(BF16)

    nt = (((1,), (1,)), ((), ()))
    qt = lax.dot_general(wqt_ref[...], h, nt, preferred_element_type=F32)
    qc, qs = qc_ref[...], qs_ref[...]
    for g in range(qt.shape[0] // HEAD_DIM):
        base = g * HEAD_DIM
        t1 = qt[base:base + ROPE_HALF]
        t2 = qt[base + ROPE_HALF:base + ROPE_DIM]
        rest = qt[base + ROPE_DIM:base + HEAD_DIM]
        grp = jnp.concatenate([t1 * qc - t2 * qs, t2 * qc + t1 * qs, rest], axis=0)
        qt_out[base:base + HEAD_DIM, :] = (grp * q_scale).astype(BF16)
    vt = lax.dot_general(wvt_ref[...], h, nt, preferred_element_type=F32)
    vt_out[...] = vt.astype(BF16)

    u = jnp.dot(h, wu_ref[...], preferred_element_type=F32)

    @pl.when(j == 0)
    def _():
        ext_ref[0:POOL_HALO, :] = jnp.zeros((POOL_HALO, ext_ref.shape[1]), F32)

    ext_ref[POOL_HALO:POOL_HALO + tm, :] = u
    pos = (j * tm + lax.broadcasted_iota(jnp.int32, (tm, 1), 0) + 1).astype(F32)
    pgw = pw_ref.shape[1]
    mixed = []
    for gi, w in enumerate(POOL_WINDOWS):
        cols = slice(gi * pgw, (gi + 1) * pgw)
        acc = u[:, cols]
        for s in range(1, w):
            acc = acc + ext_ref[POOL_HALO - s:POOL_HALO - s + tm, cols]
        mean = acc / jnp.minimum(pos, float(w))
        pooled = (mean - u[:, cols]).astype(BF16)
        mixed.append(jnp.dot(pooled, pw_ref[gi], preferred_element_type=F32))
    mixed = jnp.concatenate(mixed, axis=1) * ps_ref[...]
    y_pool = jnp.dot(mixed.astype(BF16), wpo_ref[...], preferred_element_type=F32)
    ext_ref[0:POOL_HALO, :] = ext_ref[tm:tm + POOL_HALO, :]

    gl = jnp.dot(h, wg_ref[...], preferred_element_type=F32)
    bg = bg_ref[...]
    g0 = jax.nn.sigmoid(gl[:, :d_model] + bg[0:1, :])
    g1 = jax.nn.sigmoid(gl[:, d_model:] + bg[1:2, :])
    gy_out[...] = (g0 * y_pool).astype(BF16)
    g1_out[...] = g1.astype(BF16)


def _in_proj(x, g_mix, wu, wk, wg, wqt, wvt, b_gate, pool_w, pool_scale, wpo,
             kc, ksa, ksb, qc, qs, q_scale):
    B, S, D = x.shape
    tm = PROJ_TOKENS
    nj = S // tm
    tok = lambda b, j: (b, j, 0)
    feat = lambda b, j: (b, 0, j)
    out_shapes = (
        jax.ShapeDtypeStruct((B, S, wk.shape[1]), BF16),
        jax.ShapeDtypeStruct((B, wqt.shape[0], S), BF16),
        jax.ShapeDtypeStruct((B, wvt.shape[0], S), BF16),
        jax.ShapeDtypeStruct((B, S, D), BF16),
        jax.ShapeDtypeStruct((B, S, D), BF16),
    )
    return pl.pallas_call(
        functools.partial(_in_proj_kernel, q_scale=q_scale),
        grid=(B, nj),
        in_specs=[
            pl.BlockSpec((None, tm, D), tok),
            _const_spec(g_mix.shape),
            _const_spec(wu.shape), _const_spec(wk.shape), _const_spec(wg.shape),
            _const_spec(wqt.shape), _const_spec(wvt.shape), _const_spec(b_gate.shape),
            _const_spec(pool_w.shape), _const_spec(pool_scale.shape), _const_spec(wpo.shape),
            pl.BlockSpec((tm, LANES), lambda b, j: (j, 0)),
            pl.BlockSpec((tm, LANES), lambda b, j: (j, 0)),
            pl.BlockSpec((tm, LANES), lambda b, j: (j, 0)),
            pl.BlockSpec((ROPE_HALF, tm), lambda b, j: (0, j)),
            pl.BlockSpec((ROPE_HALF, tm), lambda b, j: (0, j)),
        ],
        out_specs=(
            pl.BlockSpec((None, tm, wk.shape[1]), tok),
            pl.BlockSpec((None, wqt.shape[0], tm), feat),
            pl.BlockSpec((None, wvt.shape[0], tm), feat),
            pl.BlockSpec((None, tm, D), tok),
            pl.BlockSpec((None, tm, D), tok),
        ),
        out_shape=out_shapes,
        scratch_shapes=[pltpu.VMEM((tm + POOL_HALO, wu.shape[1]), F32)],
        compiler_params=pltpu.CompilerParams(
            dimension_semantics=("arbitrary", "arbitrary"),
            vmem_limit_bytes=VMEM_LIMIT_BYTES),
        name="in_proj",
    )(x, g_mix, wu, wk, wg, wqt, wvt, b_gate, pool_w, pool_scale, wpo, kc, ksa, ksb, qc, qs)


def _diff_attn_kernel(lam_ref, k_ref, qt_ref, vt_ref, gs_ref, o_ref,
                      wq_ref, s_ref, smax_ref, p_ref, alpha_ref, m_ref, l_ref, acc_ref, *,
                      lambda_init):
    S = k_ref.shape[0]
    tq, tk = ATT_Q, ATT_K
    nq = S // tq

    lp = lam_ref[...]
    lam = (jnp.exp(jnp.sum(lp[0:1] * lp[1:2], axis=1, keepdims=True))
           - jnp.exp(jnp.sum(lp[2:3] * lp[3:4], axis=1, keepdims=True))
           + lambda_init)

    sub_tiles = tq // tk
    row = lax.broadcasted_iota(jnp.int32, (V_DIM, tq), 0)
    diag_ok = (lax.broadcasted_iota(jnp.int32, (tk, tk), 0) // CHUNK
               <= lax.broadcasted_iota(jnp.int32, (tk, tk), 1) // CHUNK)

    def aligned(index, tile):
        start = index * tile
        return start if isinstance(start, int) else pl.multiple_of(start, tile)

    def key0(t):
        return aligned(t, tk)

    def pipeline_step(values=None, softmax=None, scores=None):
        v_t = None if values is None else vt_ref[:, pl.ds(key0(values[0]), tk)]
        k_t = None if scores is None else k_ref[pl.ds(key0(scores[0]), tk), :]
        for g in range(2 * sub_tiles):
            sub = g % sub_tiles
            cs = slice(g * tk, (g + 1) * tk)
            if values is not None and sub >= values[1]:
                pv = jnp.dot(v_t, p_ref[:, cs], preferred_element_type=F32)
                acc_ref[:, cs] = alpha_ref[:, cs] * acc_ref[:, cs] + pv
            if softmax is not None and sub >= softmax:
                m_old = m_ref[:, cs]
                m_new = jnp.maximum(m_old, smax_ref[:, cs])
                alpha = jnp.exp2(m_old - m_new)
                p = jnp.exp2(s_ref[:, cs] - m_new)
                l_ref[:, cs] = alpha * l_ref[:, cs] + jnp.sum(p, axis=0, keepdims=True)
                p_ref[:, cs] = p.astype(BF16)
                alpha_ref[:, cs] = alpha
                m_ref[:, cs] = m_new
            if scores is not None and sub >= scores[1]:
                s = jnp.dot(k_t, wq_ref[:, cs], preferred_element_type=F32)
                if scores[2] and sub == scores[1]:
                    s = jnp.where(diag_ok, s, -jnp.inf)
                s_ref[:, cs] = s
                smax_ref[:, cs] = jnp.max(s, axis=0, keepdims=True)

    def q_tile(i, has_full_steps):
        q0 = aligned(i, tq)
        qt = qt_ref[:, pl.ds(q0, tq)]
        zero = jnp.zeros_like(qt)
        wq_ref[:, 0:tq] = jnp.where(row < HEAD_DIM, qt, zero)
        wq_ref[:, tq:2 * tq] = jnp.where(row >= HEAD_DIM, qt, zero)
        m_ref[...] = jnp.full(m_ref.shape, -jnp.inf, F32)
        l_ref[...] = jnp.zeros(l_ref.shape, F32)
        acc_ref[...] = jnp.zeros(acc_ref.shape, F32)

        n_full = i * sub_tiles
        if has_full_steps:
            pipeline_step(scores=(0, 0, False))
            pipeline_step(softmax=0, scores=(1, 0, False))

            def steady_pair(u, c):
                for t in (2 * u, 2 * u + 1):
                    pipeline_step(values=(t - 2, 0), softmax=0, scores=(t, 0, False))
                return c

            lax.fori_loop(1, n_full // 2, steady_pair, 0)

        for r in range(sub_tiles + 2):
            has_values = r >= 2 or has_full_steps
            has_softmax = r <= sub_tiles and (r >= 1 or has_full_steps)
            pipeline_step(
                values=(n_full + r - 2, max(0, r - 2)) if has_values else None,
                softmax=max(0, r - 1) if has_softmax else None,
                scores=(n_full + r, r, True) if r < sub_tiles else None)

        o = acc_ref[...] / l_ref[...]
        a = o[:, :tq] - lam * o[:, tq:]
        ms = jnp.mean(a * a, axis=0, keepdims=True)
        a = a * lax.rsqrt(ms + SUBLN_EPS) * gs_ref[...] * (1.0 - lambda_init)
        o_ref[pl.ds(q0, tq), :] = a.T.astype(BF16)

    q_tile(0, False)

    def later_tile(i, c):
        q_tile(i, True)
        return c

    lax.fori_loop(1, nq, later_tile, 0)


def _diff_attn(lam_params, k, qt, vt, g_subln_col, lambda_init):
    B, S, HW = k.shape
    H = HW // LANES
    return pl.pallas_call(
        functools.partial(_diff_attn_kernel, lambda_init=lambda_init),
        grid=(B, H),
        in_specs=[
            _const_spec(lam_params.shape),
            pl.BlockSpec((None, S, LANES), lambda b, h: (b, 0, h)),
            pl.BlockSpec((None, LANES, S), lambda b, h: (b, h, 0)),
            pl.BlockSpec((None, V_DIM, S), lambda b, h: (b, h, 0)),
            _const_spec(g_subln_col.shape),
        ],
        out_specs=pl.BlockSpec((None, S, V_DIM), lambda b, h: (b, 0, h)),
        out_shape=jax.ShapeDtypeStruct((B, S, H * V_DIM), BF16),
        scratch_shapes=[
            pltpu.VMEM((LANES, 2 * ATT_Q), BF16),
            pltpu.VMEM((ATT_K, 2 * ATT_Q), F32),
            pltpu.VMEM((1, 2 * ATT_Q), F32),
            pltpu.VMEM((ATT_K, 2 * ATT_Q), BF16),
            pltpu.VMEM((1, 2 * ATT_Q), F32),
            pltpu.VMEM((1, 2 * ATT_Q), F32),
            pltpu.VMEM((1, 2 * ATT_Q), F32),
            pltpu.VMEM((V_DIM, 2 * ATT_Q), F32),
        ],
        compiler_params=pltpu.CompilerParams(
            dimension_semantics=("arbitrary", "arbitrary"),
            vmem_limit_bytes=VMEM_LIMIT_BYTES),
        name="diff_attn",
    )(lam_params, k, qt, vt, g_subln_col)


def _out_mlp_kernel(x_ref, att_ref, gy_ref, g1_ref, wao_ref, wo_ref, gm_ref, wup_ref,
                    wdn_ref, gn_ref, o_ref, *, final_norm):
    y_attn = jnp.dot(att_ref[...], wao_ref[...], preferred_element_type=F32)
    merged = gy_ref[...].astype(F32) + g1_ref[...].astype(F32) * y_attn
    x1 = x_ref[...] + jnp.dot(merged.astype(BF16), wo_ref[...], preferred_element_type=F32)
    h2 = _rms(x1, gm_ref[...], NORM_EPS).astype(BF16)
    acc = x1
    d_ff = wup_ref.shape[1]
    for c in range(d_ff // FF_CHUNK):
        cols = slice(c * FF_CHUNK, (c + 1) * FF_CHUNK)
        up = jnp.dot(h2, wup_ref[:, cols], preferred_element_type=F32)
        act = jnp.square(jnp.maximum(up, 0.0)).astype(BF16)
        acc = acc + jnp.dot(act, wdn_ref[cols, :], preferred_element_type=F32)
    o_ref[...] = _rms(acc, gn_ref[...], NORM_EPS) if final_norm else acc


def _out_mlp(x, att, gy, g1, wao, wo, g_mlp, wup, wdn, g_final, final_norm):
    B, S, D = x.shape
    tm = MLP_TOKENS
    tok = lambda b, j: (b, j, 0)
    tile = pl.BlockSpec((None, tm, D), tok)
    return pl.pallas_call(
        functools.partial(_out_mlp_kernel, final_norm=final_norm),
        grid=(B, S // tm),
        in_specs=[tile, tile, tile, tile,
                  _const_spec(wao.shape), _const_spec(wo.shape), _const_spec(g_mlp.shape),
                  _const_spec(wup.shape), _const_spec(wdn.shape), _const_spec(g_final.shape)],
        out_specs=tile,
        out_shape=jax.ShapeDtypeStruct((B, S, D), F32),
        compiler_params=pltpu.CompilerParams(
            dimension_semantics=("arbitrary", "arbitrary"),
            vmem_limit_bytes=VMEM_LIMIT_BYTES),
        name="out_mlp",
    )(x, att, gy, g1, wao, wo, g_mlp, wup, wdn, g_final)


def _rope_tables(seq):
    pos = jnp.arange(seq, dtype=F32)
    inv = ROPE_THETA ** (-jnp.arange(0, ROPE_DIM, 2, dtype=F32) / ROPE_DIM)
    ang = pos[:, None] * inv[None, :]
    cos, sin = jnp.cos(ang), jnp.sin(ang)
    ones = jnp.ones((seq, HEAD_DIM - ROPE_DIM), F32)
    zeros = jnp.zeros((seq, HEAD_DIM - ROPE_HALF), F32)
    zeros_h = jnp.zeros((seq, ROPE_HALF), F32)
    c64 = jnp.concatenate([cos, cos, ones], axis=1)
    sa64 = jnp.concatenate([-sin, zeros], axis=1)
    sb64 = jnp.concatenate([zeros_h, sin, zeros[:, :HEAD_DIM - ROPE_DIM]], axis=1)
    tile2 = lambda t: jnp.concatenate([t, t], axis=1)
    return tile2(c64), tile2(sa64), tile2(sb64), cos.T, sin.T


def kernel(x, w_in, b_gate, pool_w, pool_scale, lambda_q1, lambda_k1, lambda_q2, lambda_k2,
           g_subln, w_pool_out, w_attn_out, w_o, g_mix, g_mlp, w_up, w_down, g_final):
    B, S, D = x.shape
    depth = w_in.shape[0]
    pool_width = pool_scale.shape[1]
    qk_width = (w_in.shape[2] - pool_width - 2 * D) // 3
    o1 = pool_width
    o2 = o1 + qk_width
    o3 = o2 + qk_width
    o4 = o3 + qk_width
    kc, ksa, ksb, qc, qs = _rope_tables(S)
    q_scale = HEAD_DIM ** -0.5 * LOG2E

    for l in range(depth):
        lambda_init = 0.8 - 0.6 * math.exp(-0.3 * l)
        w = w_in[l]
        wu = w[:, :o1].astype(BF16)
        wqt = w[:, o1:o2].T.astype(BF16)
        wk = w[:, o2:o3].astype(BF16)
        wvt = w[:, o3:o4].T.astype(BF16)
        wg = w[:, o4:].astype(BF16)
        k, qt, vt, gy, g1 = _in_proj(
            x, g_mix[l][None, :], wu, wk, wg, wqt, wvt, b_gate[l],
            pool_w[l].astype(BF16), pool_scale[l][None, :], w_pool_out[l].astype(BF16),
            kc, ksa, ksb, qc, qs, q_scale)
        lam_params = jnp.stack([lambda_q1[l], lambda_k1[l], lambda_q2[l], lambda_k2[l]])
        att = _diff_attn(lam_params.astype(F32), k, qt, vt,
                         g_subln[l].astype(F32)[:, None], lambda_init)
        x = _out_mlp(x, att, gy, g1, w_attn_out[l].astype(BF16), w_o[l].astype(BF16),
                     g_mlp[l][None, :], w_up[l].astype(BF16), w_down[l].astype(BF16),
                     g_final[None, :], final_norm=(l == depth - 1))
    return x
```

```python
import functools
import math

import jax
import jax.numpy as jnp
from jax import lax
from jax.experimental import pallas as pl
from jax.experimental.pallas import tpu as pltpu

CHUNK = 64
POOL_WINDOWS = (2, 4, 8, 16)
HEAD_DIM = 64
V_DIM = 2 * HEAD_DIM
ROPE_THETA = 500000.0
ROPE_DIM = HEAD_DIM // 4
ROPE_HALF = ROPE_DIM // 2
NORM_EPS = 1e-6
SUBLN_EPS = 1e-5
LOG2E = math.log2(math.e)

LANES = 128
SUBLANES = 8
VMEM_LIMIT_BYTES = 56 * 1024 * 1024

PROJ_TOKENS = 512
MLP_TOKENS = 512
FF_CHUNK = 1024
ATT_Q = 1024
ATT_K = 512
ATT_GROUP = 256
assert ATT_Q % ATT_K == 0 and ATT_K % ATT_GROUP == 0 and ATT_GROUP % CHUNK == 0
BF16_ROWS = 16
POOL_HALO = 16

BF16 = jnp.bfloat16
F32 = jnp.float32


def _const_spec(shape):
    zeros = (0,) * len(shape)
    return pl.BlockSpec(shape, lambda *_: zeros, pipeline_mode=pl.Buffered(1))


def _rms(x, g, eps):
    ms = jnp.mean(x * x, axis=-1, keepdims=True)
    return x * lax.rsqrt(ms + eps) * g


def _in_proj_kernel(x_ref, g_ref, wu_ref, wk_ref, wg_ref, wqt_ref, wvt_ref, bg_ref,
                    pw_ref, ps_ref, wpo_ref, kc_ref, ksa_ref, ksb_ref, qc_ref, qs_ref,
                    k_out, qt_out, vt_out, gy_out, g1_out, ext_ref, *, q_scale):
    tm = x_ref.shape[0]
    d_model = x_ref.shape[1]
    j = pl.program_id(1)

    h = _rms(x_ref[...], g_ref[...], NORM_EPS).astype(BF16)

    k = jnp.dot(h, wk_ref[...], preferred_element_type=F32)
    kc, ksa, ksb = kc_ref[...], ksa_ref[...], ksb_ref[...]
    for hb in range(k.shape[1] // LANES):
        kh = k[:, hb * LANES:(hb + 1) * LANES]
        rot = (kh * kc
               + pltpu.roll(kh, LANES - ROPE_HALF, 1) * ksa
               + pltpu.roll(kh, ROPE_HALF, 1) * ksb)
        k_out[:, hb * LANES:(hb + 1) * LANES] = rot.astype(BF16)

    nt = (((1,), (1,)), ((), ()))
    qt = lax.dot_general(wqt_ref[...], h, nt, preferred_element_type=F32)
    qc, qs = qc_ref[...], qs_ref[...]
    for g in range(qt.shape[0] // HEAD_DIM):
        base = g * HEAD_DIM
        t1 = qt[base:base + ROPE_HALF]
        t2 = qt[base + ROPE_HALF:base + ROPE_DIM]
        rest = qt[base + ROPE_DIM:base + HEAD_DIM]
        grp = jnp.concatenate([t1 * qc - t2 * qs, t2 * qc + t1 * qs, rest], axis=0)
        qt_out[base:base + HEAD_DIM, :] = (grp * q_scale).astype(BF16)
    vt = lax.dot_general(wvt_ref[...], h, nt, preferred_element_type=F32)
    vt_out[...] = vt.astype(BF16)

    u = jnp.dot(h, wu_ref[...], preferred_element_type=F32)

    @pl.when(j == 0)
    def _():
        ext_ref[0:POOL_HALO, :] = jnp.zeros((POOL_HALO, ext_ref.shape[1]), F32)

    ext_ref[POOL_HALO:POOL_HALO + tm, :] = u
    pos = (j * tm + lax.broadcasted_iota(jnp.int32, (tm, 1), 0) + 1).astype(F32)
    pgw = pw_ref.shape[1]
    mixed = []
    for gi, w in enumerate(POOL_WINDOWS):
        cols = slice(gi * pgw, (gi + 1) * pgw)
        acc = u[:, cols]
        for s in range(1, w):
            acc = acc + ext_ref[POOL_HALO - s:POOL_HALO - s + tm, cols]
        mean = acc / jnp.minimum(pos, float(w))
        pooled = (mean - u[:, cols]).astype(BF16)
        mixed.append(jnp.dot(pooled, pw_ref[gi], preferred_element_type=F32))
    mixed = jnp.concatenate(mixed, axis=1) * ps_ref[...]
    y_pool = jnp.dot(mixed.astype(BF16), wpo_ref[...], preferred_element_type=F32)
    ext_ref[0:POOL_HALO, :] = ext_ref[tm:tm + POOL_HALO, :]

    gl = jnp.dot(h, wg_ref[...], preferred_element_type=F32)
    bg = bg_ref[...]
    g0 = jax.nn.sigmoid(gl[:, :d_model] + bg[0:1, :])
    g1 = jax.nn.sigmoid(gl[:, d_model:] + bg[1:2, :])
    gy_out[...] = (g0 * y_pool).astype(BF16)
    g1_out[...] = g1.astype(BF16)


def _in_proj(x, g_mix, wu, wk, wg, wqt, wvt, b_gate, pool_w, pool_scale, wpo,
             kc, ksa, ksb, qc, qs, q_scale):
    B, S, D = x.shape
    tm = PROJ_TOKENS
    nj = S // tm
    tok = lambda b, j: (b, j, 0)
    feat = lambda b, j: (b, 0, j)
    out_shapes = (
        jax.ShapeDtypeStruct((B, S, wk.shape[1]), BF16),
        jax.ShapeDtypeStruct((B, wqt.shape[0], S), BF16),
        jax.ShapeDtypeStruct((B, wvt.shape[0], S), BF16),
        jax.ShapeDtypeStruct((B, S, D), BF16),
        jax.ShapeDtypeStruct((B, S, D), BF16),
    )
    return pl.pallas_call(
        functools.partial(_in_proj_kernel, q_scale=q_scale),
        grid=(B, nj),
        in_specs=[
            pl.BlockSpec((None, tm, D), tok),
            _const_spec(g_mix.shape),
            _const_spec(wu.shape), _const_spec(wk.shape), _const_spec(wg.shape),
            _const_spec(wqt.shape), _const_spec(wvt.shape), _const_spec(b_gate.shape),
            _const_spec(pool_w.shape), _const_spec(pool_scale.shape), _const_spec(wpo.shape),
            pl.BlockSpec((tm, LANES), lambda b, j: (j, 0)),
            pl.BlockSpec((tm, LANES), lambda b, j: (j, 0)),
            pl.BlockSpec((tm, LANES), lambda b, j: (j, 0)),
            pl.BlockSpec((ROPE_HALF, tm), lambda b, j: (0, j)),
            pl.BlockSpec((ROPE_HALF, tm), lambda b, j: (0, j)),
        ],
        out_specs=(
            pl.BlockSpec((None, tm, wk.shape[1]), tok),
            pl.BlockSpec((None, wqt.shape[0], tm), feat),
            pl.BlockSpec((None, wvt.shape[0], tm), feat),
            pl.BlockSpec((None, tm, D), tok),
            pl.BlockSpec((None, tm, D), tok),
        ),
        out_shape=out_shapes,
        scratch_shapes=[pltpu.VMEM((tm + POOL_HALO, wu.shape[1]), F32)],
        compiler_params=pltpu.CompilerParams(
            dimension_semantics=("arbitrary", "arbitrary"),
            vmem_limit_bytes=VMEM_LIMIT_BYTES),
        name="in_proj",
    )(x, g_mix, wu, wk, wg, wqt, wvt, b_gate, pool_w, pool_scale, wpo, kc, ksa, ksb, qc, qs)


def _diff_attn_kernel(lam_ref, k_ref, qt_ref, vt_ref, gs_ref, o_ref,
                      wq_ref, s_ref, smax_ref, p_ref, alpha_ref, m_ref, acc_ref, *,
                      lambda_init):
    S = k_ref.shape[0]
    tq, tk = ATT_Q, ATT_K
    nq = S // tq

    lp = lam_ref[...]
    lam = (jnp.exp(jnp.sum(lp[0:1] * lp[1:2], axis=1, keepdims=True))
           - jnp.exp(jnp.sum(lp[2:3] * lp[3:4], axis=1, keepdims=True))
           + lambda_init)

    gw = ATT_GROUP
    col_subs = tq // gw
    key_blocks = tk // gw
    diag_steps = tq // tk
    row = lax.broadcasted_iota(jnp.int32, (V_DIM, tq), 0)
    chunk_gap = (lax.broadcasted_iota(jnp.int32, (tk, gw), 0) // CHUNK
                 - lax.broadcasted_iota(jnp.int32, (tk, gw), 1) // CHUNK)
    ones_rows = jnp.ones((BF16_ROWS, tk), BF16)

    def diag_mask(rel):
        return chunk_gap <= rel * (gw // CHUNK)

    def aligned(index, tile):
        start = index * tile
        return start if isinstance(start, int) else pl.multiple_of(start, tile)

    def key0(t):
        return aligned(t, tk)

    def pipeline_step(values=None, softmax=None, scores=None):
        if values is not None:
            v_t = jnp.concatenate(
                [vt_ref[:, pl.ds(key0(values[0]), tk)], ones_rows], axis=0)
        if scores is not None:
            k_t = k_ref[pl.ds(key0(scores[0]), tk), :]
        for g in range(2 * col_subs):
            sub = g % col_subs
            cs = slice(g * gw, (g + 1) * gw)
            if values is not None and sub >= values[1]:
                pv = jnp.dot(v_t, p_ref[:, cs], preferred_element_type=F32)
                acc_ref[:, cs] = alpha_ref[:, cs] * acc_ref[:, cs] + pv
            if softmax is not None and sub >= softmax:
                m_old = m_ref[:, cs]
                m_new = jnp.maximum(m_old, smax_ref[:, cs])
                alpha_ref[:, cs] = jnp.exp2(m_old - m_new)
                p_ref[:, cs] = jnp.exp2(s_ref[:, cs] - m_new).astype(BF16)
                m_ref[:, cs] = m_new
            if scores is not None and sub >= scores[1]:
                s = jnp.dot(k_t, wq_ref[:, cs], preferred_element_type=F32)
                if scores[2] and sub - scores[1] < key_blocks:
                    s = jnp.where(diag_mask(sub - scores[1]), s, -jnp.inf)
                s_ref[:, cs] = s
                smax_ref[:, cs] = jnp.max(s, axis=0, keepdims=True)

    def q_tile(i, has_full_steps):
        q0 = aligned(i, tq)
        qt = qt_ref[:, pl.ds(q0, tq)]
        zero = jnp.zeros_like(qt)
        wq_ref[:, 0:tq] = jnp.where(row < HEAD_DIM, qt, zero)
        wq_ref[:, tq:2 * tq] = jnp.where(row >= HEAD_DIM, qt, zero)
        m_ref[...] = jnp.full(m_ref.shape, -jnp.inf, F32)
        acc_ref[...] = jnp.zeros(acc_ref.shape, F32)

        n_full = i * diag_steps
        if has_full_steps:
            pipeline_step(scores=(0, 0, False))
            pipeline_step(softmax=0, scores=(1, 0, False))

            def steady_pair(u, c):
                for t in (2 * u, 2 * u + 1):
                    pipeline_step(values=(t - 2, 0), softmax=0, scores=(t, 0, False))
                return c

            lax.fori_loop(1, n_full // 2, steady_pair, 0)

        for r in range(diag_steps + 2):
            has_values = r >= 2 or has_full_steps
            has_softmax = r <= diag_steps and (r >= 1 or has_full_steps)
            pipeline_step(
                values=(n_full + r - 2, max(0, r - 2) * key_blocks) if has_values else None,
                softmax=max(0, r - 1) * key_blocks if has_softmax else None,
                scores=(n_full + r, r * key_blocks, True) if r < diag_steps else None)

        o = acc_ref[0:V_DIM, :] / acc_ref[V_DIM:V_DIM + 1, :]
        a = o[:, :tq] - lam * o[:, tq:]
        ms = jnp.mean(a * a, axis=0, keepdims=True)
        a = a * lax.rsqrt(ms + SUBLN_EPS) * gs_ref[...] * (1.0 - lambda_init)
        o_ref[pl.ds(q0, tq), :] = a.T.astype(BF16)

    q_tile(0, False)

    def later_tile(i, c):
        q_tile(i, True)
        return c

    lax.fori_loop(1, nq, later_tile, 0)


def _diff_attn(lam_params, k, qt, vt, g_subln_col, lambda_init):
    B, S, HW = k.shape
    H = HW // LANES
    return pl.pallas_call(
        functools.partial(_diff_attn_kernel, lambda_init=lambda_init),
        grid=(B, H),
        in_specs=[
            _const_spec(lam_params.shape),
            pl.BlockSpec((None, S, LANES), lambda b, h: (b, 0, h)),
            pl.BlockSpec((None, LANES, S), lambda b, h: (b, h, 0)),
            pl.BlockSpec((None, V_DIM, S), lambda b, h: (b, h, 0)),
            _const_spec(g_subln_col.shape),
        ],
        out_specs=pl.BlockSpec((None, S, V_DIM), lambda b, h: (b, 0, h)),
        out_shape=jax.ShapeDtypeStruct((B, S, H * V_DIM), BF16),
        scratch_shapes=[
            pltpu.VMEM((LANES, 2 * ATT_Q), BF16),
            pltpu.VMEM((ATT_K, 2 * ATT_Q), F32),
            pltpu.VMEM((1, 2 * ATT_Q), F32),
            pltpu.VMEM((ATT_K, 2 * ATT_Q), BF16),
            pltpu.VMEM((1, 2 * ATT_Q), F32),
            pltpu.VMEM((1, 2 * ATT_Q), F32),
            pltpu.VMEM((V_DIM + BF16_ROWS, 2 * ATT_Q), F32),
        ],
        compiler_params=pltpu.CompilerParams(
            dimension_semantics=("arbitrary", "arbitrary"),
            vmem_limit_bytes=VMEM_LIMIT_BYTES),
        name="diff_attn",
    )(lam_params, k, qt, vt, g_subln_col)


def _out_mlp_kernel(x_ref, att_ref, gy_ref, g1_ref, wao_ref, wo_ref, gm_ref, wup_ref,
                    wdn_ref, gn_ref, o_ref, *, final_norm):
    y_attn = jnp.dot(att_ref[...], wao_ref[...], preferred_element_type=F32)
    merged = gy_ref[...].astype(F32) + g1_ref[...].astype(F32) * y_attn
    x1 = x_ref[...] + jnp.dot(merged.astype(BF16), wo_ref[...], preferred_element_type=F32)
    h2 = _rms(x1, gm_ref[...], NORM_EPS).astype(BF16)
    acc = x1
    d_ff = wup_ref.shape[1]
    for c in range(d_ff // FF_CHUNK):
        cols = slice(c * FF_CHUNK, (c + 1) * FF_CHUNK)
        up = jnp.dot(h2, wup_ref[:, cols], preferred_element_type=F32)
        act = jnp.square(jnp.maximum(up, 0.0)).astype(BF16)
        acc = acc + jnp.dot(act, wdn_ref[cols, :], preferred_element_type=F32)
    o_ref[...] = _rms(acc, gn_ref[...], NORM_EPS) if final_norm else acc


def _out_mlp(x, att, gy, g1, wao, wo, g_mlp, wup, wdn, g_final, final_norm):
    B, S, D = x.shape
    tm = MLP_TOKENS
    tok = lambda b, j: (b, j, 0)
    tile = pl.BlockSpec((None, tm, D), tok)
    return pl.pallas_call(
        functools.partial(_out_mlp_kernel, final_norm=final_norm),
        grid=(B, S // tm),
        in_specs=[tile, tile, tile, tile,
                  _const_spec(wao.shape), _const_spec(wo.shape), _const_spec(g_mlp.shape),
                  _const_spec(wup.shape), _const_spec(wdn.shape), _const_spec(g_final.shape)],
        out_specs=tile,
        out_shape=jax.ShapeDtypeStruct((B, S, D), F32),
        compiler_params=pltpu.CompilerParams(
            dimension_semantics=("arbitrary", "arbitrary"),
            vmem_limit_bytes=VMEM_LIMIT_BYTES),
        name="out_mlp",
    )(x, att, gy, g1, wao, wo, g_mlp, wup, wdn, g_final)


def _rope_tables(seq):
    pos = jnp.arange(seq, dtype=F32)
    inv = ROPE_THETA ** (-jnp.arange(0, ROPE_DIM, 2, dtype=F32) / ROPE_DIM)
    ang = pos[:, None] * inv[None, :]
    cos, sin = jnp.cos(ang), jnp.sin(ang)
    ones = jnp.ones((seq, HEAD_DIM - ROPE_DIM), F32)
    zeros = jnp.zeros((seq, HEAD_DIM - ROPE_HALF), F32)
    zeros_h = jnp.zeros((seq, ROPE_HALF), F32)
    c64 = jnp.concatenate([cos, cos, ones], axis=1)
    sa64 = jnp.concatenate([-sin, zeros], axis=1)
    sb64 = jnp.concatenate([zeros_h, sin, zeros[:, :HEAD_DIM - ROPE_DIM]], axis=1)
    tile2 = lambda t: jnp.concatenate([t, t], axis=1)
    return tile2(c64), tile2(sa64), tile2(sb64), cos.T, sin.T


def kernel(x, w_in, b_gate, pool_w, pool_scale, lambda_q1, lambda_k1, lambda_q2, lambda_k2,
           g_subln, w_pool_out, w_attn_out, w_o, g_mix, g_mlp, w_up, w_down, g_final):
    B, S, D = x.shape
    depth = w_in.shape[0]
    pool_width = pool_scale.shape[1]
    qk_width = (w_in.shape[2] - pool_width - 2 * D) // 3
    o1 = pool_width
    o2 = o1 + qk_width
    o3 = o2 + qk_width
    o4 = o3 + qk_width
    kc, ksa, ksb, qc, qs = _rope_tables(S)
    q_scale = HEAD_DIM ** -0.5 * LOG2E

    for l in range(depth):
        lambda_init = 0.8 - 0.6 * math.exp(-0.3 * l)
        w = w_in[l]
        wu = w[:, :o1].astype(BF16)
        wqt = w[:, o1:o2].T.astype(BF16)
        wk = w[:, o2:o3].astype(BF16)
        wvt = w[:, o3:o4].T.astype(BF16)
        wg = w[:, o4:].astype(BF16)
        k, qt, vt, gy, g1 = _in_proj(
            x, g_mix[l][None, :], wu, wk, wg, wqt, wvt, b_gate[l],
            pool_w[l].astype(BF16), pool_scale[l][None, :], w_pool_out[l].astype(BF16),
            kc, ksa, ksb, qc, qs, q_scale)
        lam_params = jnp.stack([lambda_q1[l], lambda_k1[l], lambda_q2[l], lambda_k2[l]])
        att = _diff_attn(lam_params.astype(F32), k, qt, vt,
                         g_subln[l].astype(F32)[:, None], lambda_init)
        x = _out_mlp(x, att, gy, g1, w_attn_out[l].astype(BF16), w_o[l].astype(BF16),
                     g_mlp[l][None, :], w_up[l].astype(BF16), w_down[l].astype(BF16),
                     g_final[None, :], final_norm=(l == depth - 1))
    return x
```

```python
import functools
import math

import jax
import jax.numpy as jnp
from jax import lax
from jax.experimental import pallas as pl
from jax.experimental.pallas import tpu as pltpu

CHUNK = 64
POOL_WINDOWS = (2, 4, 8, 16)
HEAD_DIM = 64
V_DIM = 2 * HEAD_DIM
ROPE_THETA = 500000.0
ROPE_DIM = HEAD_DIM // 4
ROPE_HALF = ROPE_DIM // 2
NORM_EPS = 1e-6
SUBLN_EPS = 1e-5
LOG2E = math.log2(math.e)

LANES = 128
SUBLANES = 8
VMEM_LIMIT_BYTES = 56 * 1024 * 1024

PROJ_TOKENS = 512
MLP_TOKENS = 512
FF_CHUNK = 1024
ATT_Q = 1024
ATT_K = 512
ATT_GROUP = 256
assert ATT_Q % ATT_K == 0 and ATT_K % ATT_GROUP == 0 and ATT_GROUP % CHUNK == 0
BF16_ROWS = 16
POOL_HALO = 16

BF16 = jnp.bfloat16
F32 = jnp.float32


def _const_spec(shape):
    zeros = (0,) * len(shape)
    return pl.BlockSpec(shape, lambda *_: zeros, pipeline_mode=pl.Buffered(1))


def _rms(x, g, eps):
    ms = jnp.mean(x * x, axis=-1, keepdims=True)
    return x * lax.rsqrt(ms + eps) * g


def _in_proj_kernel(x_ref, g_ref, wu_ref, wk_ref, wg_ref, wqt_ref, wvt_ref, bg_ref,
                    pw_ref, ps_ref, wpo_ref, kc_ref, ksa_ref, ksb_ref, qc_ref, qs_ref,
                    k_out, qt_out, vt_out, gy_out, g1_out, ext_ref, g0_ref, *, q_scale):
    tm = x_ref.shape[0]
    d_model = x_ref.shape[1]
    j = pl.program_id(1)

    @pl.when(j == 0)
    def _():
        ext_ref[0:POOL_HALO, :] = jnp.zeros((POOL_HALO, ext_ref.shape[1]), F32)

    h = _rms(x_ref[...], g_ref[...], NORM_EPS).astype(BF16)

    gl = jnp.dot(h, wg_ref[...], preferred_element_type=F32)
    bg = bg_ref[...]
    g0_ref[...] = jax.nn.sigmoid(gl[:, :d_model] + bg[0:1, :])
    g1_out[...] = jax.nn.sigmoid(gl[:, d_model:] + bg[1:2, :]).astype(BF16)

    u = jnp.dot(h, wu_ref[...], preferred_element_type=F32)
    ext_ref[POOL_HALO:POOL_HALO + tm, :] = u

    k = jnp.dot(h, wk_ref[...], preferred_element_type=F32)
    kc, ksa, ksb = kc_ref[...], ksa_ref[...], ksb_ref[...]
    for hb in range(k.shape[1] // LANES):
        kh = k[:, hb * LANES:(hb + 1) * LANES]
        rot = (kh * kc
               + pltpu.roll(kh, LANES - ROPE_HALF, 1) * ksa
               + pltpu.roll(kh, ROPE_HALF, 1) * ksb)
        k_out[hb] = rot.astype(BF16)

    pos = (j * tm + lax.broadcasted_iota(jnp.int32, (tm, 1), 0) + 1).astype(F32)
    pgw = pw_ref.shape[1]
    mixed = []
    for gi, w in enumerate(POOL_WINDOWS):
        cols = slice(gi * pgw, (gi + 1) * pgw)
        acc = ext_ref[POOL_HALO:POOL_HALO + tm, cols]
        for s in range(1, w):
            acc = acc + ext_ref[POOL_HALO - s:POOL_HALO - s + tm, cols]
        mean = acc / jnp.minimum(pos, float(w))
        pooled = (mean - ext_ref[POOL_HALO:POOL_HALO + tm, cols]).astype(BF16)
        mixed.append(jnp.dot(pooled, pw_ref[gi], preferred_element_type=F32))
    mixed = (jnp.concatenate(mixed, axis=1) * ps_ref[...]).astype(BF16)
    ext_ref[0:POOL_HALO, :] = ext_ref[tm:tm + POOL_HALO, :]

    nt = (((1,), (1,)), ((), ()))
    qt = lax.dot_general(wqt_ref[...], h, nt, preferred_element_type=F32)
    qc, qs = qc_ref[...], qs_ref[...]
    for g in range(qt.shape[0] // HEAD_DIM):
        base = g * HEAD_DIM
        t1 = qt[base:base + ROPE_HALF]
        t2 = qt[base + ROPE_HALF:base + ROPE_DIM]
        rest = qt[base + ROPE_DIM:base + HEAD_DIM]
        grp = jnp.concatenate([t1 * qc - t2 * qs, t2 * qc + t1 * qs, rest], axis=0)
        qt_out[base:base + HEAD_DIM, :] = (grp * q_scale).astype(BF16)

    y_pool = jnp.dot(mixed, wpo_ref[...], preferred_element_type=F32)
    gy_out[...] = (g0_ref[...] * y_pool).astype(BF16)

    vt = lax.dot_general(wvt_ref[...], h, nt, preferred_element_type=F32)
    vt_out[...] = vt.astype(BF16)


def _in_proj(x, g_mix, wu, wk, wg, wqt, wvt, b_gate, pool_w, pool_scale, wpo,
             kc, ksa, ksb, qc, qs, q_scale):
    B, S, D = x.shape
    tm = PROJ_TOKENS
    nj = S // tm
    tok = lambda b, j: (b, j, 0)
    feat = lambda b, j: (b, 0, j)
    out_shapes = (
        jax.ShapeDtypeStruct((B, wk.shape[1] // LANES, S, LANES), BF16),
        jax.ShapeDtypeStruct((B, wqt.shape[0], S), BF16),
        jax.ShapeDtypeStruct((B, wvt.shape[0], S), BF16),
        jax.ShapeDtypeStruct((B, S, D), BF16),
        jax.ShapeDtypeStruct((B, S, D), BF16),
    )
    return pl.pallas_call(
        functools.partial(_in_proj_kernel, q_scale=q_scale),
        grid=(B, nj),
        in_specs=[
            pl.BlockSpec((None, tm, D), tok),
            _const_spec(g_mix.shape),
            _const_spec(wu.shape), _const_spec(wk.shape), _const_spec(wg.shape),
            _const_spec(wqt.shape), _const_spec(wvt.shape), _const_spec(b_gate.shape),
            _const_spec(pool_w.shape), _const_spec(pool_scale.shape), _const_spec(wpo.shape),
            pl.BlockSpec((tm, LANES), lambda b, j: (j, 0)),
            pl.BlockSpec((tm, LANES), lambda b, j: (j, 0)),
            pl.BlockSpec((tm, LANES), lambda b, j: (j, 0)),
            pl.BlockSpec((ROPE_HALF, tm), lambda b, j: (0, j)),
            pl.BlockSpec((ROPE_HALF, tm), lambda b, j: (0, j)),
        ],
        out_specs=(
            pl.BlockSpec((None, wk.shape[1] // LANES, tm, LANES), lambda b, j: (b, 0, j, 0)),
            pl.BlockSpec((None, wqt.shape[0], tm), feat),
            pl.BlockSpec((None, wvt.shape[0], tm), feat),
            pl.BlockSpec((None, tm, D), tok),
            pl.BlockSpec((None, tm, D), tok),
        ),
        out_shape=out_shapes,
        scratch_shapes=[pltpu.VMEM((tm + POOL_HALO, wu.shape[1]), F32),
                        pltpu.VMEM((tm, D), F32)],
        compiler_params=pltpu.CompilerParams(
            dimension_semantics=("arbitrary", "arbitrary"),
            vmem_limit_bytes=VMEM_LIMIT_BYTES),
        name="in_proj",
    )(x, g_mix, wu, wk, wg, wqt, wvt, b_gate, pool_w, pool_scale, wpo, kc, ksa, ksb, qc, qs)


def _diff_attn_kernel(lam_ref, k_ref, qt_ref, vt_ref, gs_ref, o_ref,
                      wq_ref, s_ref, smax_ref, p_ref, alpha_ref, m_ref, acc_ref, *,
                      lambda_init):
    S = k_ref.shape[0]
    tq, tk = ATT_Q, ATT_K
    nq = S // tq

    lp = lam_ref[...]
    lam = (jnp.exp(jnp.sum(lp[0:1] * lp[1:2], axis=1, keepdims=True))
           - jnp.exp(jnp.sum(lp[2:3] * lp[3:4], axis=1, keepdims=True))
           + lambda_init)

    gw = ATT_GROUP
    col_subs = tq // gw
    key_blocks = tk // gw
    diag_steps = tq // tk
    row = lax.broadcasted_iota(jnp.int32, (V_DIM, tq), 0)
    chunk_gap = (lax.broadcasted_iota(jnp.int32, (tk, gw), 0) // CHUNK
                 - lax.broadcasted_iota(jnp.int32, (tk, gw), 1) // CHUNK)
    ones_rows = jnp.ones((BF16_ROWS, tk), BF16)

    def diag_mask(rel):
        return chunk_gap <= rel * (gw // CHUNK)

    def aligned(index, tile):
        start = index * tile
        return start if isinstance(start, int) else pl.multiple_of(start, tile)

    def key0(t):
        return aligned(t, tk)

    def pipeline_step(values=None, softmax=None, scores=None):
        if values is not None:
            v_t = jnp.concatenate(
                [vt_ref[:, pl.ds(key0(values[0]), tk)], ones_rows], axis=0)
        if scores is not None:
            k_t = k_ref[pl.ds(key0(scores[0]), tk), :]
        for g in range(2 * col_subs):
            sub = g % col_subs
            cs = slice(g * gw, (g + 1) * gw)
            if values is not None and sub >= values[1]:
                pv = jnp.dot(v_t, p_ref[:, cs], preferred_element_type=F32)
                acc_ref[:, cs] = alpha_ref[:, cs] * acc_ref[:, cs] + pv
            if softmax is not None and sub >= softmax:
                m_old = m_ref[:, cs]
                m_new = jnp.maximum(m_old, smax_ref[:, cs])
                alpha_ref[:, cs] = jnp.exp2(m_old - m_new)
                p_ref[:, cs] = jnp.exp2(s_ref[:, cs] - m_new).astype(BF16)
                m_ref[:, cs] = m_new
            if scores is not None and sub >= scores[1]:
                s = jnp.dot(k_t, wq_ref[:, cs], preferred_element_type=F32)
                if scores[2] and sub - scores[1] < key_blocks:
                    s = jnp.where(diag_mask(sub - scores[1]), s, -jnp.inf)
                s_ref[:, cs] = s
                smax_ref[:, cs] = jnp.max(s, axis=0, keepdims=True)

    def q_tile(i, has_full_steps):
        q0 = aligned(i, tq)
        qt = qt_ref[:, pl.ds(q0, tq)]
        zero = jnp.zeros_like(qt)
        wq_ref[:, 0:tq] = jnp.where(row < HEAD_DIM, qt, zero)
        wq_ref[:, tq:2 * tq] = jnp.where(row >= HEAD_DIM, qt, zero)
        m_ref[...] = jnp.full(m_ref.shape, -jnp.inf, F32)
        acc_ref[...] = jnp.zeros(acc_ref.shape, F32)

        n_full = i * diag_steps
        if has_full_steps:
            pipeline_step(scores=(0, 0, False))
            pipeline_step(softmax=0, scores=(1, 0, False))

            def steady_pair(u, c):
                for t in (2 * u, 2 * u + 1):
                    pipeline_step(values=(t - 2, 0), softmax=0, scores=(t, 0, False))
                return c

            lax.fori_loop(1, n_full // 2, steady_pair, 0)

        for r in range(diag_steps + 2):
            has_values = r >= 2 or has_full_steps
            has_softmax = r <= diag_steps and (r >= 1 or has_full_steps)
            pipeline_step(
                values=(n_full + r - 2, max(0, r - 2) * key_blocks) if has_values else None,
                softmax=max(0, r - 1) * key_blocks if has_softmax else None,
                scores=(n_full + r, r * key_blocks, True) if r < diag_steps else None)

        o = acc_ref[0:V_DIM, :] / acc_ref[V_DIM:V_DIM + 1, :]
        a = o[:, :tq] - lam * o[:, tq:]
        ms = jnp.mean(a * a, axis=0, keepdims=True)
        a = a * lax.rsqrt(ms + SUBLN_EPS) * gs_ref[...] * (1.0 - lambda_init)
        o_ref[pl.ds(q0, tq), :] = a.T.astype(BF16)

    q_tile(0, False)

    def later_tile(i, c):
        q_tile(i, True)
        return c

    lax.fori_loop(1, nq, later_tile, 0)


def _diff_attn(lam_params, k, qt, vt, g_subln_col, lambda_init):
    B, H, S, _ = k.shape
    return pl.pallas_call(
        functools.partial(_diff_attn_kernel, lambda_init=lambda_init),
        grid=(B, H),
        in_specs=[
            _const_spec(lam_params.shape),
            pl.BlockSpec((None, None, S, LANES), lambda b, h: (b, h, 0, 0)),
            pl.BlockSpec((None, LANES, S), lambda b, h: (b, h, 0)),
            pl.BlockSpec((None, V_DIM, S), lambda b, h: (b, h, 0)),
            _const_spec(g_subln_col.shape),
        ],
        out_specs=pl.BlockSpec((None, None, S, V_DIM), lambda b, h: (b, h, 0, 0)),
        out_shape=jax.ShapeDtypeStruct((B, H, S, V_DIM), BF16),
        scratch_shapes=[
            pltpu.VMEM((LANES, 2 * ATT_Q), BF16),
            pltpu.VMEM((ATT_K, 2 * ATT_Q), F32),
            pltpu.VMEM((1, 2 * ATT_Q), F32),
            pltpu.VMEM((ATT_K, 2 * ATT_Q), BF16),
            pltpu.VMEM((1, 2 * ATT_Q), F32),
            pltpu.VMEM((1, 2 * ATT_Q), F32),
            pltpu.VMEM((V_DIM + BF16_ROWS, 2 * ATT_Q), F32),
        ],
        compiler_params=pltpu.CompilerParams(
            dimension_semantics=("arbitrary", "arbitrary"),
            vmem_limit_bytes=VMEM_LIMIT_BYTES),
        name="diff_attn",
    )(lam_params, k, qt, vt, g_subln_col)


def _out_mlp_kernel(x_ref, att_ref, gy_ref, g1_ref, wao_ref, wo_ref, gm_ref, wup_ref,
                    wdn_ref, gn_ref, o_ref, *, final_norm):
    att = jnp.concatenate([att_ref[h] for h in range(att_ref.shape[0])], axis=1)
    y_attn = jnp.dot(att, wao_ref[...], preferred_element_type=F32)
    merged = gy_ref[...].astype(F32) + g1_ref[...].astype(F32) * y_attn
    x1 = x_ref[...] + jnp.dot(merged.astype(BF16), wo_ref[...], preferred_element_type=F32)
    h2 = _rms(x1, gm_ref[...], NORM_EPS).astype(BF16)
    acc = x1
    d_ff = wup_ref.shape[1]
    for c in range(d_ff // FF_CHUNK):
        cols = slice(c * FF_CHUNK, (c + 1) * FF_CHUNK)
        up = jnp.dot(h2, wup_ref[:, cols], preferred_element_type=F32)
        act = jnp.square(jnp.maximum(up, 0.0)).astype(BF16)
        acc = acc + jnp.dot(act, wdn_ref[cols, :], preferred_element_type=F32)
    o_ref[...] = _rms(acc, gn_ref[...], NORM_EPS) if final_norm else acc


def _out_mlp(x, att, gy, g1, wao, wo, g_mlp, wup, wdn, g_final, final_norm):
    B, S, D = x.shape
    tm = MLP_TOKENS
    tok = lambda b, j: (b, j, 0)
    tile = pl.BlockSpec((None, tm, D), tok)
    return pl.pallas_call(
        functools.partial(_out_mlp_kernel, final_norm=final_norm),
        grid=(B, S // tm),
        in_specs=[tile,
                  pl.BlockSpec((None, att.shape[1], tm, att.shape[3]), lambda b, j: (b, 0, j, 0)),
                  tile, tile,
                  _const_spec(wao.shape), _const_spec(wo.shape), _const_spec(g_mlp.shape),
                  _const_spec(wup.shape), _const_spec(wdn.shape), _const_spec(g_final.shape)],
        out_specs=tile,
        out_shape=jax.ShapeDtypeStruct((B, S, D), F32),
        compiler_params=pltpu.CompilerParams(
            dimension_semantics=("arbitrary", "arbitrary"),
            vmem_limit_bytes=VMEM_LIMIT_BYTES),
        name="out_mlp",
    )(x, att, gy, g1, wao, wo, g_mlp, wup, wdn, g_final)


def _rope_tables(seq):
    pos = jnp.arange(seq, dtype=F32)
    inv = ROPE_THETA ** (-jnp.arange(0, ROPE_DIM, 2, dtype=F32) / ROPE_DIM)
    ang = pos[:, None] * inv[None, :]
    cos, sin = jnp.cos(ang), jnp.sin(ang)
    ones = jnp.ones((seq, HEAD_DIM - ROPE_DIM), F32)
    zeros = jnp.zeros((seq, HEAD_DIM - ROPE_HALF), F32)
    zeros_h = jnp.zeros((seq, ROPE_HALF), F32)
    c64 = jnp.concatenate([cos, cos, ones], axis=1)
    sa64 = jnp.concatenate([-sin, zeros], axis=1)
    sb64 = jnp.concatenate([zeros_h, sin, zeros[:, :HEAD_DIM - ROPE_DIM]], axis=1)
    tile2 = lambda t: jnp.concatenate([t, t], axis=1)
    return tile2(c64), tile2(sa64), tile2(sb64), cos.T, sin.T


def kernel(x, w_in, b_gate, pool_w, pool_scale, lambda_q1, lambda_k1, lambda_q2, lambda_k2,
           g_subln, w_pool_out, w_attn_out, w_o, g_mix, g_mlp, w_up, w_down, g_final):
    B, S, D = x.shape
    depth = w_in.shape[0]
    pool_width = pool_scale.shape[1]
    qk_width = (w_in.shape[2] - pool_width - 2 * D) // 3
    o1 = pool_width
    o2 = o1 + qk_width
    o3 = o2 + qk_width
    o4 = o3 + qk_width
    kc, ksa, ksb, qc, qs = _rope_tables(S)
    q_scale = HEAD_DIM ** -0.5 * LOG2E

    for l in range(depth):
        lambda_init = 0.8 - 0.6 * math.exp(-0.3 * l)
        w = w_in[l]
        wu = w[:, :o1].astype(BF16)
        wqt = w[:, o1:o2].T.astype(BF16)
        wk = w[:, o2:o3].astype(BF16)
        wvt = w[:, o3:o4].T.astype(BF16)
        wg = w[:, o4:].astype(BF16)
        k, qt, vt, gy, g1 = _in_proj(
            x, g_mix[l][None, :], wu, wk, wg, wqt, wvt, b_gate[l],
            pool_w[l].astype(BF16), pool_scale[l][None, :], w_pool_out[l].astype(BF16),
            kc, ksa, ksb, qc, qs, q_scale)
        lam_params = jnp.stack([lambda_q1[l], lambda_k1[l], lambda_q2[l], lambda_k2[l]])
        att = _diff_attn(lam_params.astype(F32), k, qt, vt,
                         g_subln[l].astype(F32)[:, None], lambda_init)
        x = _out_mlp(x, att, gy, g1, w_attn_out[l].astype(BF16), w_o[l].astype(BF16),
                     g_mlp[l][None, :], w_up[l].astype(BF16), w_down[l].astype(BF16),
                     g_final[None, :], final_norm=(l == depth - 1))
    return x
```

```python
import functools
import math

import jax
import jax.numpy as jnp
from jax import lax
from jax.experimental import pallas as pl
from jax.experimental.pallas import tpu as pltpu

CHUNK = 64
POOL_WINDOWS = (2, 4, 8, 16)
HEAD_DIM = 64
V_DIM = 2 * HEAD_DIM
ROPE_THETA = 500000.0
ROPE_DIM = HEAD_DIM // 4
ROPE_HALF = ROPE_DIM // 2
NORM_EPS = 1e-6
SUBLN_EPS = 1e-5
LOG2E = math.log2(math.e)

LANES = 128
SUBLANES = 8
VMEM_LIMIT_BYTES = 56 * 1024 * 1024

PROJ_TOKENS = 512
MLP_TOKENS = 512
FF_CHUNK = 1024
ATT_Q = 1024
ATT_K = 512
ATT_GROUP = 256
assert ATT_Q % ATT_K == 0 and ATT_K % ATT_GROUP == 0 and ATT_GROUP % CHUNK == 0
BF16_ROWS = 16
POOL_HALO = 16

BF16 = jnp.bfloat16
F32 = jnp.float32


def _const_spec(shape):
    zeros = (0,) * len(shape)
    return pl.BlockSpec(shape, lambda *_: zeros, pipeline_mode=pl.Buffered(1))


def _rms(x, g, eps):
    ms = jnp.mean(x * x, axis=-1, keepdims=True)
    return x * lax.rsqrt(ms + eps) * g


def _in_proj_kernel(x_ref, g_ref, wu_ref, wk_ref, wg_ref, wqt_ref, wvt_ref, bg_ref,
                    pw_ref, ps_ref, wpo_ref, kc_ref, ksa_ref, ksb_ref, qc_ref, qs_ref,
                    k_out, qt_out, vt_out, gy_out, g1_out, ext_ref, g0_ref, *, q_scale):
    tm = x_ref.shape[0]
    d_model = x_ref.shape[1]
    j = pl.program_id(1)

    @pl.when(j == 0)
    def _():
        ext_ref[0:POOL_HALO, :] = jnp.zeros((POOL_HALO, ext_ref.shape[1]), F32)

    h = _rms(x_ref[...], g_ref[...], NORM_EPS).astype(BF16)

    gl = jnp.dot(h, wg_ref[...], preferred_element_type=F32)
    bg = bg_ref[...]
    g0_ref[...] = jax.nn.sigmoid(gl[:, :d_model] + bg[0:1, :])
    g1_out[...] = jax.nn.sigmoid(gl[:, d_model:] + bg[1:2, :]).astype(BF16)

    u = jnp.dot(h, wu_ref[...], preferred_element_type=F32)
    ext_ref[POOL_HALO:POOL_HALO + tm, :] = u

    k = jnp.dot(h, wk_ref[...], preferred_element_type=F32)
    kc, ksa, ksb = kc_ref[...], ksa_ref[...], ksb_ref[...]
    for hb in range(k.shape[1] // LANES):
        kh = k[:, hb * LANES:(hb + 1) * LANES]
        rot = (kh * kc
               + pltpu.roll(kh, LANES - ROPE_HALF, 1) * ksa
               + pltpu.roll(kh, ROPE_HALF, 1) * ksb)
        k_out[hb] = rot.astype(BF16)

    pos = (j * tm + lax.broadcasted_iota(jnp.int32, (tm, 1), 0) + 1).astype(F32)
    pgw = pw_ref.shape[1]
    mixed = []
    for gi, w in enumerate(POOL_WINDOWS):
        cols = slice(gi * pgw, (gi + 1) * pgw)
        acc = ext_ref[POOL_HALO:POOL_HALO + tm, cols]
        for s in range(1, w):
            acc = acc + ext_ref[POOL_HALO - s:POOL_HALO - s + tm, cols]
        mean = acc / jnp.minimum(pos, float(w))
        pooled = (mean - ext_ref[POOL_HALO:POOL_HALO + tm, cols]).astype(BF16)
        mixed.append(jnp.dot(pooled, pw_ref[gi], preferred_element_type=F32))
    mixed = (jnp.concatenate(mixed, axis=1) * ps_ref[...]).astype(BF16)
    ext_ref[0:POOL_HALO, :] = ext_ref[tm:tm + POOL_HALO, :]

    nt = (((1,), (1,)), ((), ()))
    qt = lax.dot_general(wqt_ref[...], h, nt, preferred_element_type=F32)
    qc, qs = qc_ref[...], qs_ref[...]
    for g in range(qt.shape[0] // HEAD_DIM):
        base = g * HEAD_DIM
        t1 = qt[base:base + ROPE_HALF]
        t2 = qt[base + ROPE_HALF:base + ROPE_DIM]
        rest = qt[base + ROPE_DIM:base + HEAD_DIM]
        grp = jnp.concatenate([t1 * qc - t2 * qs, t2 * qc + t1 * qs, rest], axis=0)
        qt_out[base:base + HEAD_DIM, :] = (grp * q_scale).astype(BF16)

    y_pool = jnp.dot(mixed, wpo_ref[...], preferred_element_type=F32)
    gy_out[...] = (g0_ref[...] * y_pool).astype(BF16)

    vt = lax.dot_general(wvt_ref[...], h, nt, preferred_element_type=F32)
    vt_out[...] = vt.astype(BF16)


def _in_proj(x, g_mix, wu, wk, wg, wqt, wvt, b_gate, pool_w, pool_scale, wpo,
             kc, ksa, ksb, qc, qs, q_scale):
    B, S, D = x.shape
    tm = PROJ_TOKENS
    nj = S // tm
    tok = lambda b, j: (b, j, 0)
    feat = lambda b, j: (b, 0, j)
    out_shapes = (
        jax.ShapeDtypeStruct((B, wk.shape[1] // LANES, S, LANES), BF16),
        jax.ShapeDtypeStruct((B, wqt.shape[0], S), BF16),
        jax.ShapeDtypeStruct((B, wvt.shape[0], S), BF16),
        jax.ShapeDtypeStruct((B, S, D), BF16),
        jax.ShapeDtypeStruct((B, S, D), BF16),
    )
    return pl.pallas_call(
        functools.partial(_in_proj_kernel, q_scale=q_scale),
        grid=(B, nj),
        in_specs=[
            pl.BlockSpec((None, tm, D), tok),
            _const_spec(g_mix.shape),
            _const_spec(wu.shape), _const_spec(wk.shape), _const_spec(wg.shape),
            _const_spec(wqt.shape), _const_spec(wvt.shape), _const_spec(b_gate.shape),
            _const_spec(pool_w.shape), _const_spec(pool_scale.shape), _const_spec(wpo.shape),
            pl.BlockSpec((tm, LANES), lambda b, j: (j, 0)),
            pl.BlockSpec((tm, LANES), lambda b, j: (j, 0)),
            pl.BlockSpec((tm, LANES), lambda b, j: (j, 0)),
            pl.BlockSpec((ROPE_HALF, tm), lambda b, j: (0, j)),
            pl.BlockSpec((ROPE_HALF, tm), lambda b, j: (0, j)),
        ],
        out_specs=(
            pl.BlockSpec((None, wk.shape[1] // LANES, tm, LANES), lambda b, j: (b, 0, j, 0)),
            pl.BlockSpec((None, wqt.shape[0], tm), feat),
            pl.BlockSpec((None, wvt.shape[0], tm), feat),
            pl.BlockSpec((None, tm, D), tok),
            pl.BlockSpec((None, tm, D), tok),
        ),
        out_shape=out_shapes,
        scratch_shapes=[pltpu.VMEM((tm + POOL_HALO, wu.shape[1]), F32),
                        pltpu.VMEM((tm, D), F32)],
        compiler_params=pltpu.CompilerParams(
            dimension_semantics=("arbitrary", "arbitrary"),
            vmem_limit_bytes=VMEM_LIMIT_BYTES),
        name="in_proj",
    )(x, g_mix, wu, wk, wg, wqt, wvt, b_gate, pool_w, pool_scale, wpo, kc, ksa, ksb, qc, qs)


def _diff_attn_kernel(lam_ref, k_ref, qt_ref, vt_ref, gs_ref, o_ref,
                      wq_ref, s_ref, smax_ref, p_ref, alpha_ref, m_ref, acc_ref, *,
                      lambda_init):
    S = k_ref.shape[0]
    tq, tk = ATT_Q, ATT_K
    nq = S // tq

    lp = lam_ref[...]
    lam = (jnp.exp(jnp.sum(lp[0:1] * lp[1:2], axis=1, keepdims=True))
           - jnp.exp(jnp.sum(lp[2:3] * lp[3:4], axis=1, keepdims=True))
           + lambda_init)

    gw = ATT_GROUP
    col_subs = tq // gw
    key_blocks = tk // gw
    diag_steps = tq // tk
    row = lax.broadcasted_iota(jnp.int32, (V_DIM, tq), 0)
    chunk_gap = (lax.broadcasted_iota(jnp.int32, (tk, gw), 0) // CHUNK
                 - lax.broadcasted_iota(jnp.int32, (tk, gw), 1) // CHUNK)
    ones_rows = jnp.ones((BF16_ROWS, tk), BF16)

    def diag_mask(rel):
        return chunk_gap <= rel * (gw // CHUNK)

    def aligned(index, tile):
        start = index * tile
        return start if isinstance(start, int) else pl.multiple_of(start, tile)

    def key0(t):
        return aligned(t, tk)

    def pipeline_step(values=None, softmax=None, scores=None):
        if values is not None:
            v_t = jnp.concatenate(
                [vt_ref[:, pl.ds(key0(values[0]), tk)], ones_rows], axis=0)
        if scores is not None:
            k_t = k_ref[pl.ds(key0(scores[0]), tk), :]
        for g in range(2 * col_subs):
            sub = g % col_subs
            cs = slice(g * gw, (g + 1) * gw)
            if values is not None and sub >= values[1]:
                pv = jnp.dot(v_t, p_ref[:, cs], preferred_element_type=F32)
                acc_ref[:, cs] = alpha_ref[:, cs] * acc_ref[:, cs] + pv
            if softmax is not None and sub >= softmax:
                m_old = m_ref[:, cs]
                m_new = jnp.maximum(m_old, smax_ref[:, cs])
                alpha_ref[:, cs] = jnp.exp2(m_old - m_new)
                p_ref[:, cs] = jnp.exp2(s_ref[:, cs] - m_new).astype(BF16)
                m_ref[:, cs] = m_new
            if scores is not None and sub >= scores[1]:
                s = jnp.dot(k_t, wq_ref[:, cs], preferred_element_type=F32)
                if scores[2] and sub - scores[1] < key_blocks:
                    s = jnp.where(diag_mask(sub - scores[1]), s, -jnp.inf)
                s_ref[:, cs] = s
                smax_ref[:, cs] = jnp.max(s, axis=0, keepdims=True)

    def q_tile(i, has_full_steps):
        q0 = aligned(i, tq)
        qt = qt_ref[:, pl.ds(q0, tq)]
        zero = jnp.zeros_like(qt)
        wq_ref[:, 0:tq] = jnp.where(row < HEAD_DIM, qt, zero)
        wq_ref[:, tq:2 * tq] = jnp.where(row >= HEAD_DIM, qt, zero)
        m_ref[...] = jnp.full(m_ref.shape, -jnp.inf, F32)
        acc_ref[...] = jnp.zeros(acc_ref.shape, F32)

        n_full = i * diag_steps
        if has_full_steps:
            pipeline_step(scores=(0, 0, False))
            pipeline_step(softmax=0, scores=(1, 0, False))

            def steady_slots(t0, count):
                for t in range(count):
                    pipeline_step(values=(t0 + t - 2, 0), softmax=0,
                                  scores=(t0 + t, 0, False))

            n_pairs = n_full // 2 - 1

            def steady_quad(u, c):
                steady_slots(2 + 4 * u, 4)
                return c

            lax.fori_loop(0, n_pairs // 2, steady_quad, 0)

            @pl.when(n_pairs % 2 == 1)
            def _():
                steady_slots(2 + 4 * (n_pairs // 2), 2)

        for r in range(diag_steps + 2):
            has_values = r >= 2 or has_full_steps
            has_softmax = r <= diag_steps and (r >= 1 or has_full_steps)
            pipeline_step(
                values=(n_full + r - 2, max(0, r - 2) * key_blocks) if has_values else None,
                softmax=max(0, r - 1) * key_blocks if has_softmax else None,
                scores=(n_full + r, r * key_blocks, True) if r < diag_steps else None)

        o = acc_ref[0:V_DIM, :] / acc_ref[V_DIM:V_DIM + 1, :]
        a = o[:, :tq] - lam * o[:, tq:]
        ms = jnp.mean(a * a, axis=0, keepdims=True)
        a = a * lax.rsqrt(ms + SUBLN_EPS) * gs_ref[...] * (1.0 - lambda_init)
        o_ref[pl.ds(q0, tq), :] = a.T.astype(BF16)

    q_tile(0, False)

    def later_tile(i, c):
        q_tile(i, True)
        return c

    lax.fori_loop(1, nq, later_tile, 0)


def _diff_attn(lam_params, k, qt, vt, g_subln_col, lambda_init):
    B, H, S, _ = k.shape
    return pl.pallas_call(
        functools.partial(_diff_attn_kernel, lambda_init=lambda_init),
        grid=(B, H),
        in_specs=[
            _const_spec(lam_params.shape),
            pl.BlockSpec((None, None, S, LANES), lambda b, h: (b, h, 0, 0)),
            pl.BlockSpec((None, LANES, S), lambda b, h: (b, h, 0)),
            pl.BlockSpec((None, V_DIM, S), lambda b, h: (b, h, 0)),
            _const_spec(g_subln_col.shape),
        ],
        out_specs=pl.BlockSpec((None, None, S, V_DIM), lambda b, h: (b, h, 0, 0)),
        out_shape=jax.ShapeDtypeStruct((B, H, S, V_DIM), BF16),
        scratch_shapes=[
            pltpu.VMEM((LANES, 2 * ATT_Q), BF16),
            pltpu.VMEM((ATT_K, 2 * ATT_Q), F32),
            pltpu.VMEM((1, 2 * ATT_Q), F32),
            pltpu.VMEM((ATT_K, 2 * ATT_Q), BF16),
            pltpu.VMEM((1, 2 * ATT_Q), F32),
            pltpu.VMEM((1, 2 * ATT_Q), F32),
            pltpu.VMEM((V_DIM + BF16_ROWS, 2 * ATT_Q), F32),
        ],
        compiler_params=pltpu.CompilerParams(
            dimension_semantics=("arbitrary", "arbitrary"),
            vmem_limit_bytes=VMEM_LIMIT_BYTES),
        name="diff_attn",
    )(lam_params, k, qt, vt, g_subln_col)


def _out_mlp_kernel(x_ref, att_ref, gy_ref, g1_ref, wao_ref, wo_ref, gm_ref, wup_ref,
                    wdn_ref, gn_ref, o_ref, *, final_norm):
    att = jnp.concatenate([att_ref[h] for h in range(att_ref.shape[0])], axis=1)
    y_attn = jnp.dot(att, wao_ref[...], preferred_element_type=F32)
    merged = gy_ref[...].astype(F32) + g1_ref[...].astype(F32) * y_attn
    x1 = x_ref[...] + jnp.dot(merged.astype(BF16), wo_ref[...], preferred_element_type=F32)
    h2 = _rms(x1, gm_ref[...], NORM_EPS).astype(BF16)
    acc = x1
    d_ff = wup_ref.shape[1]
    for c in range(d_ff // FF_CHUNK):
        cols = slice(c * FF_CHUNK, (c + 1) * FF_CHUNK)
        up = jnp.dot(h2, wup_ref[:, cols], preferred_element_type=F32)
        act = jnp.square(jnp.maximum(up, 0.0)).astype(BF16)
        acc = acc + jnp.dot(act, wdn_ref[cols, :], preferred_element_type=F32)
    o_ref[...] = _rms(acc, gn_ref[...], NORM_EPS) if final_norm else acc


def _out_mlp(x, att, gy, g1, wao, wo, g_mlp, wup, wdn, g_final, final_norm):
    B, S, D = x.shape
    tm = MLP_TOKENS
    tok = lambda b, j: (b, j, 0)
    tile = pl.BlockSpec((None, tm, D), tok)
    return pl.pallas_call(
        functools.partial(_out_mlp_kernel, final_norm=final_norm),
        grid=(B, S // tm),
        in_specs=[tile,
                  pl.BlockSpec((None, att.shape[1], tm, att.shape[3]), lambda b, j: (b, 0, j, 0)),
                  tile, tile,
                  _const_spec(wao.shape), _const_spec(wo.shape), _const_spec(g_mlp.shape),
                  _const_spec(wup.shape), _const_spec(wdn.shape), _const_spec(g_final.shape)],
        out_specs=tile,
        out_shape=jax.ShapeDtypeStruct((B, S, D), F32),
        compiler_params=pltpu.CompilerParams(
            dimension_semantics=("arbitrary", "arbitrary"),
            vmem_limit_bytes=VMEM_LIMIT_BYTES),
        name="out_mlp",
    )(x, att, gy, g1, wao, wo, g_mlp, wup, wdn, g_final)


def _rope_tables(seq):
    pos = jnp.arange(seq, dtype=F32)
    inv = ROPE_THETA ** (-jnp.arange(0, ROPE_DIM, 2, dtype=F32) / ROPE_DIM)
    ang_t = inv[:, None] * pos[None, :]
    d = jnp.arange(LANES) % HEAD_DIM
    freq = jnp.where(d < ROPE_DIM, inv[d % ROPE_HALF], 0.0)
    ang = pos[:, None] * freq[None, :]
    cos, sin = jnp.cos(ang), jnp.sin(ang)
    sa = jnp.where(d < ROPE_HALF, -sin, 0.0)
    sb = jnp.where((d >= ROPE_HALF) & (d < ROPE_DIM), sin, 0.0)
    return cos, sa, sb, jnp.cos(ang_t), jnp.sin(ang_t)


def kernel(x, w_in, b_gate, pool_w, pool_scale, lambda_q1, lambda_k1, lambda_q2, lambda_k2,
           g_subln, w_pool_out, w_attn_out, w_o, g_mix, g_mlp, w_up, w_down, g_final):
    B, S, D = x.shape
    depth = w_in.shape[0]
    pool_width = pool_scale.shape[1]
    qk_width = (w_in.shape[2] - pool_width - 2 * D) // 3
    o1 = pool_width
    o2 = o1 + qk_width
    o3 = o2 + qk_width
    o4 = o3 + qk_width
    kc, ksa, ksb, qc, qs = _rope_tables(S)
    q_scale = HEAD_DIM ** -0.5 * LOG2E

    for l in range(depth):
        lambda_init = 0.8 - 0.6 * math.exp(-0.3 * l)
        w = w_in[l]
        wu = w[:, :o1].astype(BF16)
        wqt = w[:, o1:o2].T.astype(BF16)
        wk = w[:, o2:o3].astype(BF16)
        wvt = w[:, o3:o4].T.astype(BF16)
        wg = w[:, o4:].astype(BF16)
        k, qt, vt, gy, g1 = _in_proj(
            x, g_mix[l][None, :], wu, wk, wg, wqt, wvt, b_gate[l],
            pool_w[l].astype(BF16), pool_scale[l][None, :], w_pool_out[l].astype(BF16),
            kc, ksa, ksb, qc, qs, q_scale)
        lam_params = jnp.stack([lambda_q1[l], lambda_k1[l], lambda_q2[l], lambda_k2[l]])
        att = _diff_attn(lam_params.astype(F32), k, qt, vt,
                         g_subln[l].astype(F32)[:, None], lambda_init)
        x = _out_mlp(x, att, gy, g1, w_attn_out[l].astype(BF16), w_o[l].astype(BF16),
                     g_mlp[l][None, :], w_up[l].astype(BF16), w_down[l].astype(BF16),
                     g_final[None, :], final_norm=(l == depth - 1))
    return x
```

```python
import functools
import math

import jax
import jax.numpy as jnp
from jax import lax
from jax.experimental import pallas as pl
from jax.experimental.pallas import tpu as pltpu

CHUNK = 64
POOL_WINDOWS = (2, 4, 8, 16)
HEAD_DIM = 64
V_DIM = 2 * HEAD_DIM
ROPE_THETA = 500000.0
ROPE_DIM = HEAD_DIM // 4
ROPE_HALF = ROPE_DIM // 2
NORM_EPS = 1e-6
SUBLN_EPS = 1e-5
LOG2E = math.log2(math.e)

LANES = 128
SUBLANES = 8
VMEM_LIMIT_BYTES = 56 * 1024 * 1024

PROJ_TOKENS = 512
MLP_TOKENS = 512
FF_CHUNK = 1024
ATT_Q = 1024
ATT_K = 512
ATT_GROUP = 256
assert ATT_Q % ATT_K == 0 and ATT_K % ATT_GROUP == 0 and ATT_GROUP % CHUNK == 0
BF16_ROWS = 16
POOL_HALO = 16

BF16 = jnp.bfloat16
F32 = jnp.float32


def _const_spec(shape):
    zeros = (0,) * len(shape)
    return pl.BlockSpec(shape, lambda *_: zeros, pipeline_mode=pl.Buffered(1))


def _rms(x, g, eps):
    ms = jnp.mean(x * x, axis=-1, keepdims=True)
    return x * lax.rsqrt(ms + eps) * g


def _in_proj_kernel(x_ref, g_ref, wu_ref, wk_ref, wg_ref, wqt_ref, wvt_ref, bg_ref,
                    pw_ref, ps_ref, wpo_ref, kc_ref, ksa_ref, ksb_ref, qc_ref, qs_ref,
                    k_out, qt_out, vt_out, gy_out, g1_out, ext_ref, g0_ref, *, q_scale):
    tm = x_ref.shape[0]
    d_model = x_ref.shape[1]
    j = pl.program_id(1)

    @pl.when(j == 0)
    def _():
        ext_ref[0:POOL_HALO, :] = jnp.zeros((POOL_HALO, ext_ref.shape[1]), F32)

    h = _rms(x_ref[...], g_ref[...], NORM_EPS).astype(BF16)

    gl = jnp.dot(h, wg_ref[...], preferred_element_type=F32)
    bg = bg_ref[...]
    g0_ref[...] = jax.nn.sigmoid(gl[:, :d_model] + bg[0:1, :])
    g1_out[...] = jax.nn.sigmoid(gl[:, d_model:] + bg[1:2, :]).astype(BF16)

    u = jnp.dot(h, wu_ref[...], preferred_element_type=F32)
    ext_ref[POOL_HALO:POOL_HALO + tm, :] = u

    k = jnp.dot(h, wk_ref[...], preferred_element_type=F32)
    kc, ksa, ksb = kc_ref[...], ksa_ref[...], ksb_ref[...]
    for hb in range(k.shape[1] // LANES):
        kh = k[:, hb * LANES:(hb + 1) * LANES]
        rot = (kh * kc
               + pltpu.roll(kh, LANES - ROPE_HALF, 1) * ksa
               + pltpu.roll(kh, ROPE_HALF, 1) * ksb)
        k_out[hb] = rot.astype(BF16)

    pos = (j * tm + lax.broadcasted_iota(jnp.int32, (tm, 1), 0) + 1).astype(F32)
    pgw = pw_ref.shape[1]
    mixed = []
    for gi, w in enumerate(POOL_WINDOWS):
        cols = slice(gi * pgw, (gi + 1) * pgw)
        acc = ext_ref[POOL_HALO:POOL_HALO + tm, cols]
        for s in range(1, w):
            acc = acc + ext_ref[POOL_HALO - s:POOL_HALO - s + tm, cols]
        mean = acc / jnp.minimum(pos, float(w))
        pooled = (mean - ext_ref[POOL_HALO:POOL_HALO + tm, cols]).astype(BF16)
        mixed.append(jnp.dot(pooled, pw_ref[gi], preferred_element_type=F32))
    mixed = (jnp.concatenate(mixed, axis=1) * ps_ref[...]).astype(BF16)
    ext_ref[0:POOL_HALO, :] = ext_ref[tm:tm + POOL_HALO, :]

    nt = (((1,), (1,)), ((), ()))
    qt = lax.dot_general(wqt_ref[...], h, nt, preferred_element_type=F32)
    qc, qs = qc_ref[...], qs_ref[...]
    for g in range(qt.shape[0] // HEAD_DIM):
        base = g * HEAD_DIM
        t1 = qt[base:base + ROPE_HALF]
        t2 = qt[base + ROPE_HALF:base + ROPE_DIM]
        rest = qt[base + ROPE_DIM:base + HEAD_DIM]
        grp = jnp.concatenate([t1 * qc - t2 * qs, t2 * qc + t1 * qs, rest], axis=0)
        qt_out[base:base + HEAD_DIM, :] = (grp * q_scale).astype(BF16)

    y_pool = jnp.dot(mixed, wpo_ref[...], preferred_element_type=F32)
    gy_out[...] = (g0_ref[...] * y_pool).astype(BF16)

    vt = lax.dot_general(wvt_ref[...], h, nt, preferred_element_type=F32)
    vt_out[...] = vt.astype(BF16)


def _in_proj(x, g_mix, wu, wk, wg, wqt, wvt, b_gate, pool_w, pool_scale, wpo,
             kc, ksa, ksb, qc, qs, q_scale):
    B, S, D = x.shape
    tm = PROJ_TOKENS
    nj = S // tm
    tok = lambda b, j: (b, j, 0)
    feat = lambda b, j: (b, 0, j)
    out_shapes = (
        jax.ShapeDtypeStruct((B, wk.shape[1] // LANES, S, LANES), BF16),
        jax.ShapeDtypeStruct((B, wqt.shape[0], S), BF16),
        jax.ShapeDtypeStruct((B, wvt.shape[0], S), BF16),
        jax.ShapeDtypeStruct((B, S, D), BF16),
        jax.ShapeDtypeStruct((B, S, D), BF16),
    )
    return pl.pallas_call(
        functools.partial(_in_proj_kernel, q_scale=q_scale),
        grid=(B, nj),
        in_specs=[
            pl.BlockSpec((None, tm, D), tok),
            _const_spec(g_mix.shape),
            _const_spec(wu.shape), _const_spec(wk.shape), _const_spec(wg.shape),
            _const_spec(wqt.shape), _const_spec(wvt.shape), _const_spec(b_gate.shape),
            _const_spec(pool_w.shape), _const_spec(pool_scale.shape), _const_spec(wpo.shape),
            pl.BlockSpec((tm, LANES), lambda b, j: (j, 0)),
            pl.BlockSpec((tm, LANES), lambda b, j: (j, 0)),
            pl.BlockSpec((tm, LANES), lambda b, j: (j, 0)),
            pl.BlockSpec((ROPE_HALF, tm), lambda b, j: (0, j)),
            pl.BlockSpec((ROPE_HALF, tm), lambda b, j: (0, j)),
        ],
        out_specs=(
            pl.BlockSpec((None, wk.shape[1] // LANES, tm, LANES), lambda b, j: (b, 0, j, 0)),
            pl.BlockSpec((None, wqt.shape[0], tm), feat),
            pl.BlockSpec((None, wvt.shape[0], tm), feat),
            pl.BlockSpec((None, tm, D), tok),
            pl.BlockSpec((None, tm, D), tok),
        ),
        out_shape=out_shapes,
        scratch_shapes=[pltpu.VMEM((tm + POOL_HALO, wu.shape[1]), F32),
                        pltpu.VMEM((tm, D), F32)],
        compiler_params=pltpu.CompilerParams(
            dimension_semantics=("arbitrary", "arbitrary"),
            vmem_limit_bytes=VMEM_LIMIT_BYTES),
        name="in_proj",
    )(x, g_mix, wu, wk, wg, wqt, wvt, b_gate, pool_w, pool_scale, wpo, kc, ksa, ksb, qc, qs)


def _diff_attn_kernel(lam_ref, k_ref, qt_ref, vt_ref, gs_ref, o_ref,
                      wq_ref, s_ref, smax_ref, p_ref, alpha_ref, m_ref, acc_ref, done_ref, *,
                      lambda_init):
    S = k_ref.shape[0]
    tq, tk = ATT_Q, ATT_K
    nq = S // tq

    lp = lam_ref[...]
    lam = (jnp.exp(jnp.sum(lp[0:1] * lp[1:2], axis=1, keepdims=True))
           - jnp.exp(jnp.sum(lp[2:3] * lp[3:4], axis=1, keepdims=True))
           + lambda_init)

    gw = ATT_GROUP
    col_subs = tq // gw
    key_blocks = tk // gw
    diag_steps = tq // tk
    row = lax.broadcasted_iota(jnp.int32, (V_DIM, tq), 0)
    chunk_gap = (lax.broadcasted_iota(jnp.int32, (tk, gw), 0) // CHUNK
                 - lax.broadcasted_iota(jnp.int32, (tk, gw), 1) // CHUNK)
    ones_rows = jnp.ones((BF16_ROWS, tk), BF16)

    def diag_mask(rel):
        return chunk_gap <= rel * (gw // CHUNK)

    def aligned(index, tile):
        start = index * tile
        return start if isinstance(start, int) else pl.multiple_of(start, tile)

    def key0(t):
        return aligned(t, tk)

    def pipeline_step(values=None, softmax=None, scores=None):
        if values is not None:
            v_t = jnp.concatenate(
                [vt_ref[:, pl.ds(key0(values[0]), tk)], ones_rows], axis=0)
        if scores is not None:
            k_t = k_ref[pl.ds(key0(scores[0]), tk), :]
        for g in range(2 * col_subs):
            sub = g % col_subs
            cs = slice(g * gw, (g + 1) * gw)
            if values is not None and sub >= values[1]:
                pv = jnp.dot(v_t, p_ref[:, cs], preferred_element_type=F32)
                acc_ref[:, cs] = alpha_ref[:, cs] * acc_ref[:, cs] + pv
            if softmax is not None and sub >= softmax:
                m_old = m_ref[:, cs]
                m_new = jnp.maximum(m_old, smax_ref[:, cs])
                alpha_ref[:, cs] = jnp.exp2(m_old - m_new)
                p_ref[:, cs] = jnp.exp2(s_ref[:, cs] - m_new).astype(BF16)
                m_ref[:, cs] = m_new
            if scores is not None and sub >= scores[1]:
                s = jnp.dot(k_t, wq_ref[:, cs], preferred_element_type=F32)
                if scores[2] and sub - scores[1] < key_blocks:
                    s = jnp.where(diag_mask(sub - scores[1]), s, -jnp.inf)
                s_ref[:, cs] = s
                smax_ref[:, cs] = jnp.max(s, axis=0, keepdims=True)

    def set_tile_weights(i):
        qt = qt_ref[:, pl.ds(aligned(i, tq), tq)]
        zero = jnp.zeros_like(qt)
        wq_ref[:, 0:tq] = jnp.where(row < HEAD_DIM, qt, zero)
        wq_ref[:, tq:2 * tq] = jnp.where(row >= HEAD_DIM, qt, zero)

    def finalize(i, src_ref):
        inv_l = 1.0 / src_ref[V_DIM:V_DIM + 1, :]
        o = src_ref[0:V_DIM, :] * inv_l
        a = o[:, :tq] - lam * o[:, tq:]
        ms = jnp.mean(a * a, axis=0, keepdims=True)
        a = a * lax.rsqrt(ms + SUBLN_EPS) * gs_ref[...] * (1.0 - lambda_init)
        o_ref[pl.ds(aligned(i, tq), tq), :] = a.T.astype(BF16)

    def diag_slots(i, has_prev):
        n_full = i * diag_steps
        for r in range(diag_steps):
            pipeline_step(
                values=(n_full + r - 2, 0) if has_prev else None,
                softmax=0 if (has_prev or r >= 1) else None,
                scores=(n_full + r, r * key_blocks, True))

    def reset_max():
        m_ref[...] = jnp.full(m_ref.shape, -jnp.inf, F32)

    assert diag_steps == 2
    last_first = (diag_steps - 1) * key_blocks

    set_tile_weights(0)
    reset_max()
    acc_ref[...] = jnp.zeros(acc_ref.shape, F32)
    diag_slots(0, has_prev=False)

    def later_tile(i, c):
        n_prev = (i - 1) * diag_steps
        n_full = i * diag_steps
        set_tile_weights(i)
        pipeline_step(values=(n_prev, 0), softmax=last_first, scores=(0, 0, False))
        reset_max()
        pipeline_step(values=(n_prev + 1, last_first), softmax=0, scores=(1, 0, False))
        done_ref[...] = acc_ref[...]

        def steady_slots(t0, count):
            for t in range(count):
                pipeline_step(values=(t0 + t - 2, 0), softmax=0, scores=(t0 + t, 0, False))

        n_pairs = n_full // 2 - 1

        def steady_quad(u, cc):
            steady_slots(2 + 4 * u, 4)
            return cc

        lax.fori_loop(0, n_pairs // 2, steady_quad, 0)

        @pl.when(n_pairs % 2 == 1)
        def _():
            steady_slots(2 + 4 * (n_pairs // 2), 2)

        diag_slots(i, has_prev=True)
        finalize(i - 1, done_ref)
        return c

    lax.fori_loop(1, nq, later_tile, 0)

    n_last = (nq - 1) * diag_steps
    pipeline_step(values=(n_last, 0), softmax=last_first)
    pipeline_step(values=(n_last + 1, last_first))
    finalize(nq - 1, acc_ref)


def _diff_attn(lam_params, k, qt, vt, g_subln_col, lambda_init):
    B, H, S, _ = k.shape
    return pl.pallas_call(
        functools.partial(_diff_attn_kernel, lambda_init=lambda_init),
        grid=(B, H),
        in_specs=[
            _const_spec(lam_params.shape),
            pl.BlockSpec((None, None, S, LANES), lambda b, h: (b, h, 0, 0)),
            pl.BlockSpec((None, LANES, S), lambda b, h: (b, h, 0)),
            pl.BlockSpec((None, V_DIM, S), lambda b, h: (b, h, 0)),
            _const_spec(g_subln_col.shape),
        ],
        out_specs=pl.BlockSpec((None, None, S, V_DIM), lambda b, h: (b, h, 0, 0)),
        out_shape=jax.ShapeDtypeStruct((B, H, S, V_DIM), BF16),
        scratch_shapes=[
            pltpu.VMEM((LANES, 2 * ATT_Q), BF16),
            pltpu.VMEM((ATT_K, 2 * ATT_Q), F32),
            pltpu.VMEM((1, 2 * ATT_Q), F32),
            pltpu.VMEM((ATT_K, 2 * ATT_Q), BF16),
            pltpu.VMEM((1, 2 * ATT_Q), F32),
            pltpu.VMEM((1, 2 * ATT_Q), F32),
            pltpu.VMEM((V_DIM + BF16_ROWS, 2 * ATT_Q), F32),
            pltpu.VMEM((V_DIM + BF16_ROWS, 2 * ATT_Q), F32),
        ],
        compiler_params=pltpu.CompilerParams(
            dimension_semantics=("arbitrary", "arbitrary"),
            vmem_limit_bytes=VMEM_LIMIT_BYTES),
        name="diff_attn",
    )(lam_params, k, qt, vt, g_subln_col)


def _out_mlp_kernel(x_ref, att_ref, gy_ref, g1_ref, wao_ref, wo_ref, gm_ref, wup_ref,
                    wdn_ref, gn_ref, o_ref, *, final_norm):
    att = jnp.concatenate([att_ref[h] for h in range(att_ref.shape[0])], axis=1)
    y_attn = jnp.dot(att, wao_ref[...], preferred_element_type=F32)
    merged = gy_ref[...].astype(F32) + g1_ref[...].astype(F32) * y_attn
    x1 = x_ref[...] + jnp.dot(merged.astype(BF16), wo_ref[...], preferred_element_type=F32)
    h2 = _rms(x1, gm_ref[...], NORM_EPS).astype(BF16)
    acc = x1
    d_ff = wup_ref.shape[1]
    for c in range(d_ff // FF_CHUNK):
        cols = slice(c * FF_CHUNK, (c + 1) * FF_CHUNK)
        up = jnp.dot(h2, wup_ref[:, cols], preferred_element_type=F32)
        act = jnp.square(jnp.maximum(up, 0.0)).astype(BF16)
        acc = acc + jnp.dot(act, wdn_ref[cols, :], preferred_element_type=F32)
    o_ref[...] = _rms(acc, gn_ref[...], NORM_EPS) if final_norm else acc


def _out_mlp(x, att, gy, g1, wao, wo, g_mlp, wup, wdn, g_final, final_norm):
    B, S, D = x.shape
    tm = MLP_TOKENS
    tok = lambda b, j: (b, j, 0)
    tile = pl.BlockSpec((None, tm, D), tok)
    return pl.pallas_call(
        functools.partial(_out_mlp_kernel, final_norm=final_norm),
        grid=(B, S // tm),
        in_specs=[tile,
                  pl.BlockSpec((None, att.shape[1], tm, att.shape[3]), lambda b, j: (b, 0, j, 0)),
                  tile, tile,
                  _const_spec(wao.shape), _const_spec(wo.shape), _const_spec(g_mlp.shape),
                  _const_spec(wup.shape), _const_spec(wdn.shape), _const_spec(g_final.shape)],
        out_specs=tile,
        out_shape=jax.ShapeDtypeStruct((B, S, D), F32),
        compiler_params=pltpu.CompilerParams(
            dimension_semantics=("arbitrary", "arbitrary"),
            vmem_limit_bytes=VMEM_LIMIT_BYTES),
        name="out_mlp",
    )(x, att, gy, g1, wao, wo, g_mlp, wup, wdn, g_final)


def _rope_tables(seq):
    pos = jnp.arange(seq, dtype=F32)
    inv = ROPE_THETA ** (-jnp.arange(0, ROPE_DIM, 2, dtype=F32) / ROPE_DIM)
    ang_t = inv[:, None] * pos[None, :]
    d = jnp.arange(LANES) % HEAD_DIM
    freq = jnp.where(d < ROPE_DIM, inv[d % ROPE_HALF], 0.0)
    ang = pos[:, None] * freq[None, :]
    cos, sin = jnp.cos(ang), jnp.sin(ang)
    sa = jnp.where(d < ROPE_HALF, -sin, 0.0)
    sb = jnp.where((d >= ROPE_HALF) & (d < ROPE_DIM), sin, 0.0)
    return cos, sa, sb, jnp.cos(ang_t), jnp.sin(ang_t)


def kernel(x, w_in, b_gate, pool_w, pool_scale, lambda_q1, lambda_k1, lambda_q2, lambda_k2,
           g_subln, w_pool_out, w_attn_out, w_o, g_mix, g_mlp, w_up, w_down, g_final):
    B, S, D = x.shape
    depth = w_in.shape[0]
    pool_width = pool_scale.shape[1]
    qk_width = (w_in.shape[2] - pool_width - 2 * D) // 3
    o1 = pool_width
    o2 = o1 + qk_width
    o3 = o2 + qk_width
    o4 = o3 + qk_width
    kc, ksa, ksb, qc, qs = _rope_tables(S)
    q_scale = HEAD_DIM ** -0.5 * LOG2E

    for l in range(depth):
        lambda_init = 0.8 - 0.6 * math.exp(-0.3 * l)
        w = w_in[l]
        wu = w[:, :o1].astype(BF16)
        wqt = w[:, o1:o2].T.astype(BF16)
        wk = w[:, o2:o3].astype(BF16)
        wvt = w[:, o3:o4].T.astype(BF16)
        wg = w[:, o4:].astype(BF16)
        k, qt, vt, gy, g1 = _in_proj(
            x, g_mix[l][None, :], wu, wk, wg, wqt, wvt, b_gate[l],
            pool_w[l].astype(BF16), pool_scale[l][None, :], w_pool_out[l].astype(BF16),
            kc, ksa, ksb, qc, qs, q_scale)
        lam_params = jnp.stack([lambda_q1[l], lambda_k1[l], lambda_q2[l], lambda_k2[l]])
        att = _diff_attn(lam_params.astype(F32), k, qt, vt,
                         g_subln[l].astype(F32)[:, None], lambda_init)
        x = _out_mlp(x, att, gy, g1, w_attn_out[l].astype(BF16), w_o[l].astype(BF16),
                     g_mlp[l][None, :], w_up[l].astype(BF16), w_down[l].astype(BF16),
                     g_final[None, :], final_norm=(l == depth - 1))
    return x
```

```python
import functools
import math

import jax
import jax.numpy as jnp
from jax import lax
from jax.experimental import pallas as pl
from jax.experimental.pallas import tpu as pltpu

CHUNK = 64
POOL_WINDOWS = (2, 4, 8, 16)
HEAD_DIM = 64
V_DIM = 2 * HEAD_DIM
ROPE_THETA = 500000.0
ROPE_DIM = HEAD_DIM // 4
ROPE_HALF = ROPE_DIM // 2
NORM_EPS = 1e-6
SUBLN_EPS = 1e-5
LOG2E = math.log2(math.e)

LANES = 128
SUBLANES = 8
VMEM_LIMIT_BYTES = 56 * 1024 * 1024

PROJ_TOKENS = 512
MLP_TOKENS = 512
FF_CHUNK = 1024
ATT_Q = 2048
ATT_K = 512
ATT_GROUP = 256
assert ATT_Q % ATT_K == 0 and ATT_K % ATT_GROUP == 0 and ATT_GROUP % CHUNK == 0
BF16_ROWS = 16
POOL_HALO = 16

BF16 = jnp.bfloat16
F32 = jnp.float32


def _const_spec(shape):
    zeros = (0,) * len(shape)
    return pl.BlockSpec(shape, lambda *_: zeros, pipeline_mode=pl.Buffered(1))


def _rms(x, g, eps):
    ms = jnp.mean(x * x, axis=-1, keepdims=True)
    return x * lax.rsqrt(ms + eps) * g


def _in_proj_kernel(x_ref, g_ref, wu_ref, wk_ref, wg_ref, wqt_ref, wvt_ref, bg_ref,
                    pw_ref, ps_ref, wpo_ref, kc_ref, ksa_ref, ksb_ref, qc_ref, qs_ref,
                    k_out, qt_out, vt_out, gy_out, g1_out, ext_ref, g0_ref, *, q_scale):
    tm = x_ref.shape[0]
    d_model = x_ref.shape[1]
    j = pl.program_id(1)

    @pl.when(j == 0)
    def _():
        ext_ref[0:POOL_HALO, :] = jnp.zeros((POOL_HALO, ext_ref.shape[1]), F32)

    h = _rms(x_ref[...], g_ref[...], NORM_EPS).astype(BF16)

    gl = jnp.dot(h, wg_ref[...], preferred_element_type=F32)
    bg = bg_ref[...]
    g0_ref[...] = jax.nn.sigmoid(gl[:, :d_model] + bg[0:1, :])
    g1_out[...] = jax.nn.sigmoid(gl[:, d_model:] + bg[1:2, :]).astype(BF16)

    u = jnp.dot(h, wu_ref[...], preferred_element_type=F32)
    ext_ref[POOL_HALO:POOL_HALO + tm, :] = u

    k = jnp.dot(h, wk_ref[...], preferred_element_type=F32)
    kc, ksa, ksb = kc_ref[...], ksa_ref[...], ksb_ref[...]
    for hb in range(k.shape[1] // LANES):
        kh = k[:, hb * LANES:(hb + 1) * LANES]
        rot = (kh * kc
               + pltpu.roll(kh, LANES - ROPE_HALF, 1) * ksa
               + pltpu.roll(kh, ROPE_HALF, 1) * ksb)
        k_out[hb] = rot.astype(BF16)

    pos = (j * tm + lax.broadcasted_iota(jnp.int32, (tm, 1), 0) + 1).astype(F32)
    pgw = pw_ref.shape[1]
    mixed = []
    for gi, w in enumerate(POOL_WINDOWS):
        cols = slice(gi * pgw, (gi + 1) * pgw)
        acc = ext_ref[POOL_HALO:POOL_HALO + tm, cols]
        for s in range(1, w):
            acc = acc + ext_ref[POOL_HALO - s:POOL_HALO - s + tm, cols]
        mean = acc / jnp.minimum(pos, float(w))
        pooled = (mean - ext_ref[POOL_HALO:POOL_HALO + tm, cols]).astype(BF16)
        mixed.append(jnp.dot(pooled, pw_ref[gi], preferred_element_type=F32))
    mixed = (jnp.concatenate(mixed, axis=1) * ps_ref[...]).astype(BF16)
    ext_ref[0:POOL_HALO, :] = ext_ref[tm:tm + POOL_HALO, :]

    nt = (((1,), (1,)), ((), ()))
    qt = lax.dot_general(wqt_ref[...], h, nt, preferred_element_type=F32)
    qc, qs = qc_ref[...], qs_ref[...]
    for g in range(qt.shape[0] // HEAD_DIM):
        base = g * HEAD_DIM
        t1 = qt[base:base + ROPE_HALF]
        t2 = qt[base + ROPE_HALF:base + ROPE_DIM]
        rest = qt[base + ROPE_DIM:base + HEAD_DIM]
        grp = jnp.concatenate([t1 * qc - t2 * qs, t2 * qc + t1 * qs, rest], axis=0)
        qt_out[base:base + HEAD_DIM, :] = (grp * q_scale).astype(BF16)

    y_pool = jnp.dot(mixed, wpo_ref[...], preferred_element_type=F32)
    gy_out[...] = (g0_ref[...] * y_pool).astype(BF16)

    vt = lax.dot_general(wvt_ref[...], h, nt, preferred_element_type=F32)
    vt_out[...] = vt.astype(BF16)


def _in_proj(x, g_mix, wu, wk, wg, wqt, wvt, b_gate, pool_w, pool_scale, wpo,
             kc, ksa, ksb, qc, qs, q_scale):
    B, S, D = x.shape
    tm = PROJ_TOKENS
    nj = S // tm
    tok = lambda b, j: (b, j, 0)
    feat = lambda b, j: (b, 0, j)
    out_shapes = (
        jax.ShapeDtypeStruct((B, wk.shape[1] // LANES, S, LANES), BF16),
        jax.ShapeDtypeStruct((B, wqt.shape[0], S), BF16),
        jax.ShapeDtypeStruct((B, wvt.shape[0], S), BF16),
        jax.ShapeDtypeStruct((B, S, D), BF16),
        jax.ShapeDtypeStruct((B, S, D), BF16),
    )
    return pl.pallas_call(
        functools.partial(_in_proj_kernel, q_scale=q_scale),
        grid=(B, nj),
        in_specs=[
            pl.BlockSpec((None, tm, D), tok),
            _const_spec(g_mix.shape),
            _const_spec(wu.shape), _const_spec(wk.shape), _const_spec(wg.shape),
            _const_spec(wqt.shape), _const_spec(wvt.shape), _const_spec(b_gate.shape),
            _const_spec(pool_w.shape), _const_spec(pool_scale.shape), _const_spec(wpo.shape),
            pl.BlockSpec((tm, LANES), lambda b, j: (j, 0)),
            pl.BlockSpec((tm, LANES), lambda b, j: (j, 0)),
            pl.BlockSpec((tm, LANES), lambda b, j: (j, 0)),
            pl.BlockSpec((ROPE_HALF, tm), lambda b, j: (0, j)),
            pl.BlockSpec((ROPE_HALF, tm), lambda b, j: (0, j)),
        ],
        out_specs=(
            pl.BlockSpec((None, wk.shape[1] // LANES, tm, LANES), lambda b, j: (b, 0, j, 0)),
            pl.BlockSpec((None, wqt.shape[0], tm), feat),
            pl.BlockSpec((None, wvt.shape[0], tm), feat),
            pl.BlockSpec((None, tm, D), tok),
            pl.BlockSpec((None, tm, D), tok),
        ),
        out_shape=out_shapes,
        scratch_shapes=[pltpu.VMEM((tm + POOL_HALO, wu.shape[1]), F32),
                        pltpu.VMEM((tm, D), F32)],
        compiler_params=pltpu.CompilerParams(
            dimension_semantics=("arbitrary", "arbitrary"),
            vmem_limit_bytes=VMEM_LIMIT_BYTES),
        name="in_proj",
    )(x, g_mix, wu, wk, wg, wqt, wvt, b_gate, pool_w, pool_scale, wpo, kc, ksa, ksb, qc, qs)


def _diff_attn_kernel(lam_ref, k_ref, qt_ref, vt_ref, gs_ref, o_ref,
                      wq_ref, s_ref, smax_ref, p_ref, alpha_ref, m_ref, acc_ref, *,
                      lambda_init):
    S = k_ref.shape[0]
    tq, tk = ATT_Q, ATT_K
    nq = S // tq

    lp = lam_ref[...]
    lam = (jnp.exp(jnp.sum(lp[0:1] * lp[1:2], axis=1, keepdims=True))
           - jnp.exp(jnp.sum(lp[2:3] * lp[3:4], axis=1, keepdims=True))
           + lambda_init)

    gw = ATT_GROUP
    col_subs = tq // gw
    key_blocks = tk // gw
    diag_steps = tq // tk
    row = lax.broadcasted_iota(jnp.int32, (V_DIM, tq), 0)
    chunk_gap = (lax.broadcasted_iota(jnp.int32, (tk, gw), 0) // CHUNK
                 - lax.broadcasted_iota(jnp.int32, (tk, gw), 1) // CHUNK)
    ones_rows = jnp.ones((BF16_ROWS, tk), BF16)

    def diag_mask(rel):
        return chunk_gap <= rel * (gw // CHUNK)

    def aligned(index, tile):
        start = index * tile
        return start if isinstance(start, int) else pl.multiple_of(start, tile)

    def key0(t):
        return aligned(t, tk)

    def pipeline_step(values=None, softmax=None, scores=None):
        if values is not None:
            v_t = jnp.concatenate(
                [vt_ref[:, pl.ds(key0(values[0]), tk)], ones_rows], axis=0)
        if scores is not None:
            k_t = k_ref[pl.ds(key0(scores[0]), tk), :]
        for g in range(2 * col_subs):
            sub = g % col_subs
            cs = slice(g * gw, (g + 1) * gw)
            if values is not None and sub >= values[1]:
                pv = jnp.dot(v_t, p_ref[:, cs], preferred_element_type=F32)
                acc_ref[:, cs] = alpha_ref[:, cs] * acc_ref[:, cs] + pv
            if softmax is not None and sub >= softmax:
                m_old = m_ref[:, cs]
                m_new = jnp.maximum(m_old, smax_ref[:, cs])
                alpha_ref[:, cs] = jnp.exp2(m_old - m_new)
                p_ref[:, cs] = jnp.exp2(s_ref[:, cs] - m_new).astype(BF16)
                m_ref[:, cs] = m_new
            if scores is not None and sub >= scores[1]:
                s = jnp.dot(k_t, wq_ref[:, cs], preferred_element_type=F32)
                if scores[2] and sub - scores[1] < key_blocks:
                    s = jnp.where(diag_mask(sub - scores[1]), s, -jnp.inf)
                s_ref[:, cs] = s
                smax_ref[:, cs] = jnp.max(s, axis=0, keepdims=True)

    def q_tile(i, has_full_steps):
        q0 = aligned(i, tq)
        qt = qt_ref[:, pl.ds(q0, tq)]
        zero = jnp.zeros_like(qt)
        wq_ref[:, 0:tq] = jnp.where(row < HEAD_DIM, qt, zero)
        wq_ref[:, tq:2 * tq] = jnp.where(row >= HEAD_DIM, qt, zero)
        m_ref[...] = jnp.full(m_ref.shape, -jnp.inf, F32)
        acc_ref[...] = jnp.zeros(acc_ref.shape, F32)

        n_full = i * diag_steps
        if has_full_steps:
            pipeline_step(scores=(0, 0, False))
            pipeline_step(softmax=0, scores=(1, 0, False))

            def steady_slots(t0, count):
                for t in range(count):
                    pipeline_step(values=(t0 + t - 2, 0), softmax=0,
                                  scores=(t0 + t, 0, False))

            n_pairs = n_full // 2 - 1

            def steady_quad(u, c):
                steady_slots(2 + 4 * u, 4)
                return c

            lax.fori_loop(0, n_pairs // 2, steady_quad, 0)

            @pl.when(n_pairs % 2 == 1)
            def _():
                steady_slots(2 + 4 * (n_pairs // 2), 2)

        for r in range(diag_steps + 2):
            has_values = r >= 2 or has_full_steps
            has_softmax = r <= diag_steps and (r >= 1 or has_full_steps)
            pipeline_step(
                values=(n_full + r - 2, max(0, r - 2) * key_blocks) if has_values else None,
                softmax=max(0, r - 1) * key_blocks if has_softmax else None,
                scores=(n_full + r, r * key_blocks, True) if r < diag_steps else None)

        o = acc_ref[0:V_DIM, :] / acc_ref[V_DIM:V_DIM + 1, :]
        a = o[:, :tq] - lam * o[:, tq:]
        ms = jnp.mean(a * a, axis=0, keepdims=True)
        a = a * lax.rsqrt(ms + SUBLN_EPS) * gs_ref[...] * (1.0 - lambda_init)
        o_ref[pl.ds(q0, tq), :] = a.T.astype(BF16)

    q_tile(0, False)

    def later_tile(i, c):
        q_tile(i, True)
        return c

    lax.fori_loop(1, nq, later_tile, 0)


def _diff_attn(lam_params, k, qt, vt, g_subln_col, lambda_init):
    B, H, S, _ = k.shape
    return pl.pallas_call(
        functools.partial(_diff_attn_kernel, lambda_init=lambda_init),
        grid=(B, H),
        in_specs=[
            _const_spec(lam_params.shape),
            pl.BlockSpec((None, None, S, LANES), lambda b, h: (b, h, 0, 0)),
            pl.BlockSpec((None, LANES, S), lambda b, h: (b, h, 0)),
            pl.BlockSpec((None, V_DIM, S), lambda b, h: (b, h, 0)),
            _const_spec(g_subln_col.shape),
        ],
        out_specs=pl.BlockSpec((None, None, S, V_DIM), lambda b, h: (b, h, 0, 0)),
        out_shape=jax.ShapeDtypeStruct((B, H, S, V_DIM), BF16),
        scratch_shapes=[
            pltpu.VMEM((LANES, 2 * ATT_Q), BF16),
            pltpu.VMEM((ATT_K, 2 * ATT_Q), F32),
            pltpu.VMEM((1, 2 * ATT_Q), F32),
            pltpu.VMEM((ATT_K, 2 * ATT_Q), BF16),
            pltpu.VMEM((1, 2 * ATT_Q), F32),
            pltpu.VMEM((1, 2 * ATT_Q), F32),
            pltpu.VMEM((V_DIM + BF16_ROWS, 2 * ATT_Q), F32),
        ],
        compiler_params=pltpu.CompilerParams(
            dimension_semantics=("arbitrary", "arbitrary"),
            vmem_limit_bytes=VMEM_LIMIT_BYTES),
        name="diff_attn",
    )(lam_params, k, qt, vt, g_subln_col)


def _out_mlp_kernel(x_ref, att_ref, gy_ref, g1_ref, wao_ref, wo_ref, gm_ref, wup_ref,
                    wdn_ref, gn_ref, o_ref, *, final_norm):
    att = jnp.concatenate([att_ref[h] for h in range(att_ref.shape[0])], axis=1)
    y_attn = jnp.dot(att, wao_ref[...], preferred_element_type=F32)
    merged = gy_ref[...].astype(F32) + g1_ref[...].astype(F32) * y_attn
    x1 = x_ref[...] + jnp.dot(merged.astype(BF16), wo_ref[...], preferred_element_type=F32)
    h2 = _rms(x1, gm_ref[...], NORM_EPS).astype(BF16)
    acc = x1
    d_ff = wup_ref.shape[1]
    for c in range(d_ff // FF_CHUNK):
        cols = slice(c * FF_CHUNK, (c + 1) * FF_CHUNK)
        up = jnp.dot(h2, wup_ref[:, cols], preferred_element_type=F32)
        act = jnp.square(jnp.maximum(up, 0.0)).astype(BF16)
        acc = acc + jnp.dot(act, wdn_ref[cols, :], preferred_element_type=F32)
    o_ref[...] = _rms(acc, gn_ref[...], NORM_EPS) if final_norm else acc


def _out_mlp(x, att, gy, g1, wao, wo, g_mlp, wup, wdn, g_final, final_norm):
    B, S, D = x.shape
    tm = MLP_TOKENS
    tok = lambda b, j: (b, j, 0)
    tile = pl.BlockSpec((None, tm, D), tok)
    return pl.pallas_call(
        functools.partial(_out_mlp_kernel, final_norm=final_norm),
        grid=(B, S // tm),
        in_specs=[tile,
                  pl.BlockSpec((None, att.shape[1], tm, att.shape[3]), lambda b, j: (b, 0, j, 0)),
                  tile, tile,
                  _const_spec(wao.shape), _const_spec(wo.shape), _const_spec(g_mlp.shape),
                  _const_spec(wup.shape), _const_spec(wdn.shape), _const_spec(g_final.shape)],
        out_specs=tile,
        out_shape=jax.ShapeDtypeStruct((B, S, D), F32),
        compiler_params=pltpu.CompilerParams(
            dimension_semantics=("arbitrary", "arbitrary"),
            vmem_limit_bytes=VMEM_LIMIT_BYTES),
        name="out_mlp",
    )(x, att, gy, g1, wao, wo, g_mlp, wup, wdn, g_final)


def _rope_tables(seq):
    pos = jnp.arange(seq, dtype=F32)
    inv = ROPE_THETA ** (-jnp.arange(0, ROPE_DIM, 2, dtype=F32) / ROPE_DIM)
    ang_t = inv[:, None] * pos[None, :]
    d = jnp.arange(LANES) % HEAD_DIM
    freq = jnp.where(d < ROPE_DIM, inv[d % ROPE_HALF], 0.0)
    ang = pos[:, None] * freq[None, :]
    cos, sin = jnp.cos(ang), jnp.sin(ang)
    sa = jnp.where(d < ROPE_HALF, -sin, 0.0)
    sb = jnp.where((d >= ROPE_HALF) & (d < ROPE_DIM), sin, 0.0)
    return cos, sa, sb, jnp.cos(ang_t), jnp.sin(ang_t)


def kernel(x, w_in, b_gate, pool_w, pool_scale, lambda_q1, lambda_k1, lambda_q2, lambda_k2,
           g_subln, w_pool_out, w_attn_out, w_o, g_mix, g_mlp, w_up, w_down, g_final):
    B, S, D = x.shape
    depth = w_in.shape[0]
    pool_width = pool_scale.shape[1]
    qk_width = (w_in.shape[2] - pool_width - 2 * D) // 3
    o1 = pool_width
    o2 = o1 + qk_width
    o3 = o2 + qk_width
    o4 = o3 + qk_width
    kc, ksa, ksb, qc, qs = _rope_tables(S)
    q_scale = HEAD_DIM ** -0.5 * LOG2E

    for l in range(depth):
        lambda_init = 0.8 - 0.6 * math.exp(-0.3 * l)
        w = w_in[l]
        wu = w[:, :o1].astype(BF16)
        wqt = w[:, o1:o2].T.astype(BF16)
        wk = w[:, o2:o3].astype(BF16)
        wvt = w[:, o3:o4].T.astype(BF16)
        wg = w[:, o4:].astype(BF16)
        k, qt, vt, gy, g1 = _in_proj(
            x, g_mix[l][None, :], wu, wk, wg, wqt, wvt, b_gate[l],
            pool_w[l].astype(BF16), pool_scale[l][None, :], w_pool_out[l].astype(BF16),
            kc, ksa, ksb, qc, qs, q_scale)
        lam_params = jnp.stack([lambda_q1[l], lambda_k1[l], lambda_q2[l], lambda_k2[l]])
        att = _diff_attn(lam_params.astype(F32), k, qt, vt,
                         g_subln[l].astype(F32)[:, None], lambda_init)
        x = _out_mlp(x, att, gy, g1, w_attn_out[l].astype(BF16), w_o[l].astype(BF16),
                     g_mlp[l][None, :], w_up[l].astype(BF16), w_down[l].astype(BF16),
                     g_final[None, :], final_norm=(l == depth - 1))
    return x
```

```python
import functools
import math

import jax
import jax.numpy as jnp
from jax import lax
from jax.experimental import pallas as pl
from jax.experimental.pallas import tpu as pltpu

CHUNK = 64
POOL_WINDOWS = (2, 4, 8, 16)
HEAD_DIM = 64
V_DIM = 2 * HEAD_DIM
ROPE_THETA = 500000.0
ROPE_DIM = HEAD_DIM // 4
ROPE_HALF = ROPE_DIM // 2
NORM_EPS = 1e-6
SUBLN_EPS = 1e-5
LOG2E = math.log2(math.e)

LANES = 128
SUBLANES = 8
VMEM_LIMIT_BYTES = 56 * 1024 * 1024

PROJ_TOKENS = 512
MLP_TOKENS = 512
FF_CHUNK = 1024
ATT_Q = 2048
ATT_K = 512
ATT_GROUP = 256
assert ATT_Q % ATT_K == 0 and ATT_K % ATT_GROUP == 0 and ATT_GROUP % CHUNK == 0
BF16_ROWS = 16
POOL_HALO = 16

BF16 = jnp.bfloat16
F32 = jnp.float32


def _const_spec(shape):
    zeros = (0,) * len(shape)
    return pl.BlockSpec(shape, lambda *_: zeros, pipeline_mode=pl.Buffered(1))


def _rms(x, g, eps):
    ms = jnp.mean(x * x, axis=-1, keepdims=True)
    return x * lax.rsqrt(ms + eps) * g


def _in_proj_kernel(x_ref, g_ref, wu_ref, wk_ref, wg_ref, wqt_ref, wvt_ref, bg_ref,
                    pw_ref, ps_ref, wpo_ref, kc_ref, ksa_ref, ksb_ref, qc_ref, qs_ref,
                    k_out, qt_out, vt_out, gy_out, g1_out, ext_ref, g0_ref, wpool_ref, *,
                    q_scale):
    tm = x_ref.shape[0]
    d_model = x_ref.shape[1]
    j = pl.program_id(1)
    pgw = pw_ref.shape[1]

    @pl.when((pl.program_id(0) == 0) & (j == 0))
    def _():
        for gi in range(len(POOL_WINDOWS)):
            rows = slice(gi * pgw, (gi + 1) * pgw)
            scaled = pw_ref[gi].astype(F32) * ps_ref[:, rows]
            wpool_ref[rows, :] = jnp.dot(scaled, wpo_ref[rows, :].astype(F32),
                                         preferred_element_type=F32).astype(BF16)

    @pl.when(j == 0)
    def _():
        ext_ref[0:POOL_HALO, :] = jnp.zeros((POOL_HALO, ext_ref.shape[1]), F32)

    h = _rms(x_ref[...], g_ref[...], NORM_EPS).astype(BF16)

    gl = jnp.dot(h, wg_ref[...], preferred_element_type=F32)
    bg = bg_ref[...]
    g0_ref[...] = jax.nn.sigmoid(gl[:, :d_model] + bg[0:1, :])
    g1_out[...] = jax.nn.sigmoid(gl[:, d_model:] + bg[1:2, :]).astype(BF16)

    u = jnp.dot(h, wu_ref[...], preferred_element_type=F32)
    ext_ref[POOL_HALO:POOL_HALO + tm, :] = u

    k = jnp.dot(h, wk_ref[...], preferred_element_type=F32)
    kc, ksa, ksb = kc_ref[...], ksa_ref[...], ksb_ref[...]
    for hb in range(k.shape[1] // LANES):
        kh = k[:, hb * LANES:(hb + 1) * LANES]
        rot = (kh * kc
               + pltpu.roll(kh, LANES - ROPE_HALF, 1) * ksa
               + pltpu.roll(kh, ROPE_HALF, 1) * ksb)
        k_out[hb] = rot.astype(BF16)

    pos = (j * tm + lax.broadcasted_iota(jnp.int32, (tm, 1), 0) + 1).astype(F32)
    pooled = []
    for gi, w in enumerate(POOL_WINDOWS):
        cols = slice(gi * pgw, (gi + 1) * pgw)
        acc = ext_ref[POOL_HALO:POOL_HALO + tm, cols]
        for s in range(1, w):
            acc = acc + ext_ref[POOL_HALO - s:POOL_HALO - s + tm, cols]
        mean = acc / jnp.minimum(pos, float(w))
        pooled.append((mean - ext_ref[POOL_HALO:POOL_HALO + tm, cols]).astype(BF16))
    pooled = jnp.concatenate(pooled, axis=1)
    ext_ref[0:POOL_HALO, :] = ext_ref[tm:tm + POOL_HALO, :]

    nt = (((1,), (1,)), ((), ()))
    qt = lax.dot_general(wqt_ref[...], h, nt, preferred_element_type=F32)
    qc, qs = qc_ref[...], qs_ref[...]
    for g in range(qt.shape[0] // HEAD_DIM):
        base = g * HEAD_DIM
        t1 = qt[base:base + ROPE_HALF]
        t2 = qt[base + ROPE_HALF:base + ROPE_DIM]
        rest = qt[base + ROPE_DIM:base + HEAD_DIM]
        grp = jnp.concatenate([t1 * qc - t2 * qs, t2 * qc + t1 * qs, rest], axis=0)
        qt_out[base:base + HEAD_DIM, :] = (grp * q_scale).astype(BF16)

    y_pool = jnp.dot(pooled, wpool_ref[...], preferred_element_type=F32)
    gy_out[...] = (g0_ref[...] * y_pool).astype(BF16)

    vt = lax.dot_general(wvt_ref[...], h, nt, preferred_element_type=F32)
    vt_out[...] = vt.astype(BF16)


def _in_proj(x, g_mix, wu, wk, wg, wqt, wvt, b_gate, pool_w, pool_scale, wpo,
             kc, ksa, ksb, qc, qs, q_scale):
    B, S, D = x.shape
    tm = PROJ_TOKENS
    nj = S // tm
    tok = lambda b, j: (b, j, 0)
    feat = lambda b, j: (b, 0, j)
    out_shapes = (
        jax.ShapeDtypeStruct((B, wk.shape[1] // LANES, S, LANES), BF16),
        jax.ShapeDtypeStruct((B, wqt.shape[0], S), BF16),
        jax.ShapeDtypeStruct((B, wvt.shape[0], S), BF16),
        jax.ShapeDtypeStruct((B, S, D), BF16),
        jax.ShapeDtypeStruct((B, S, D), BF16),
    )
    return pl.pallas_call(
        functools.partial(_in_proj_kernel, q_scale=q_scale),
        grid=(B, nj),
        in_specs=[
            pl.BlockSpec((None, tm, D), tok),
            _const_spec(g_mix.shape),
            _const_spec(wu.shape), _const_spec(wk.shape), _const_spec(wg.shape),
            _const_spec(wqt.shape), _const_spec(wvt.shape), _const_spec(b_gate.shape),
            _const_spec(pool_w.shape), _const_spec(pool_scale.shape), _const_spec(wpo.shape),
            pl.BlockSpec((tm, LANES), lambda b, j: (j, 0)),
            pl.BlockSpec((tm, LANES), lambda b, j: (j, 0)),
            pl.BlockSpec((tm, LANES), lambda b, j: (j, 0)),
            pl.BlockSpec((ROPE_HALF, tm), lambda b, j: (0, j)),
            pl.BlockSpec((ROPE_HALF, tm), lambda b, j: (0, j)),
        ],
        out_specs=(
            pl.BlockSpec((None, wk.shape[1] // LANES, tm, LANES), lambda b, j: (b, 0, j, 0)),
            pl.BlockSpec((None, wqt.shape[0], tm), feat),
            pl.BlockSpec((None, wvt.shape[0], tm), feat),
            pl.BlockSpec((None, tm, D), tok),
            pl.BlockSpec((None, tm, D), tok),
        ),
        out_shape=out_shapes,
        scratch_shapes=[pltpu.VMEM((tm + POOL_HALO, wu.shape[1]), F32),
                        pltpu.VMEM((tm, D), F32),
                        pltpu.VMEM(wpo.shape, BF16)],
        compiler_params=pltpu.CompilerParams(
            dimension_semantics=("arbitrary", "arbitrary"),
            vmem_limit_bytes=VMEM_LIMIT_BYTES),
        name="in_proj",
    )(x, g_mix, wu, wk, wg, wqt, wvt, b_gate, pool_w, pool_scale, wpo, kc, ksa, ksb, qc, qs)


def _diff_attn_kernel(lam_ref, k_ref, qt_ref, vt_ref, gs_ref, o_ref,
                      wq_ref, s_ref, smax_ref, p_ref, alpha_ref, m_ref, acc_ref, *,
                      lambda_init):
    S = k_ref.shape[0]
    tq, tk = ATT_Q, ATT_K
    nq = S // tq

    lp = lam_ref[...]
    lam = (jnp.exp(jnp.sum(lp[0:1] * lp[1:2], axis=1, keepdims=True))
           - jnp.exp(jnp.sum(lp[2:3] * lp[3:4], axis=1, keepdims=True))
           + lambda_init)

    gw = ATT_GROUP
    col_subs = tq // gw
    key_blocks = tk // gw
    diag_steps = tq // tk
    row = lax.broadcasted_iota(jnp.int32, (V_DIM, tq), 0)
    chunk_gap = (lax.broadcasted_iota(jnp.int32, (tk, gw), 0) // CHUNK
                 - lax.broadcasted_iota(jnp.int32, (tk, gw), 1) // CHUNK)
    ones_rows = jnp.ones((BF16_ROWS, tk), BF16)

    def diag_mask(rel):
        return chunk_gap <= rel * (gw // CHUNK)

    def aligned(index, tile):
        start = index * tile
        return start if isinstance(start, int) else pl.multiple_of(start, tile)

    def key0(t):
        return aligned(t, tk)

    def pipeline_step(values=None, softmax=None, scores=None):
        if values is not None:
            v_t = jnp.concatenate(
                [vt_ref[:, pl.ds(key0(values[0]), tk)], ones_rows], axis=0)
        if scores is not None:
            k_t = k_ref[pl.ds(key0(scores[0]), tk), :]
        for g in range(2 * col_subs):
            sub = g % col_subs
            cs = slice(g * gw, (g + 1) * gw)
            if values is not None and sub >= values[1]:
                pv = jnp.dot(v_t, p_ref[:, cs], preferred_element_type=F32)
                acc_ref[:, cs] = alpha_ref[:, cs] * acc_ref[:, cs] + pv
            if softmax is not None and sub >= softmax:
                m_old = m_ref[:, cs]
                m_new = jnp.maximum(m_old, smax_ref[:, cs])
                alpha_ref[:, cs] = jnp.exp2(m_old - m_new)
                p_ref[:, cs] = jnp.exp2(s_ref[:, cs] - m_new).astype(BF16)
                m_ref[:, cs] = m_new
            if scores is not None and sub >= scores[1]:
                s = jnp.dot(k_t, wq_ref[:, cs], preferred_element_type=F32)
                if scores[2] and sub - scores[1] < key_blocks:
                    s = jnp.where(diag_mask(sub - scores[1]), s, -jnp.inf)
                s_ref[:, cs] = s
                smax_ref[:, cs] = jnp.max(s, axis=0, keepdims=True)

    def q_tile(i, has_full_steps):
        q0 = aligned(i, tq)
        qt = qt_ref[:, pl.ds(q0, tq)]
        zero = jnp.zeros_like(qt)
        wq_ref[:, 0:tq] = jnp.where(row < HEAD_DIM, qt, zero)
        wq_ref[:, tq:2 * tq] = jnp.where(row >= HEAD_DIM, qt, zero)
        m_ref[...] = jnp.full(m_ref.shape, -jnp.inf, F32)
        acc_ref[...] = jnp.zeros(acc_ref.shape, F32)

        n_full = i * diag_steps
        if has_full_steps:
            pipeline_step(scores=(0, 0, False))
            pipeline_step(softmax=0, scores=(1, 0, False))

            def steady_slots(t0, count):
                for t in range(count):
                    pipeline_step(values=(t0 + t - 2, 0), softmax=0,
                                  scores=(t0 + t, 0, False))

            n_pairs = n_full // 2 - 1

            def steady_quad(u, c):
                steady_slots(2 + 4 * u, 4)
                return c

            lax.fori_loop(0, n_pairs // 2, steady_quad, 0)

            @pl.when(n_pairs % 2 == 1)
            def _():
                steady_slots(2 + 4 * (n_pairs // 2), 2)

        for r in range(diag_steps + 2):
            has_values = r >= 2 or has_full_steps
            has_softmax = r <= diag_steps and (r >= 1 or has_full_steps)
            pipeline_step(
                values=(n_full + r - 2, max(0, r - 2) * key_blocks) if has_values else None,
                softmax=max(0, r - 1) * key_blocks if has_softmax else None,
                scores=(n_full + r, r * key_blocks, True) if r < diag_steps else None)

        o = acc_ref[0:V_DIM, :] / acc_ref[V_DIM:V_DIM + 1, :]
        a = o[:, :tq] - lam * o[:, tq:]
        ms = jnp.mean(a * a, axis=0, keepdims=True)
        a = a * lax.rsqrt(ms + SUBLN_EPS) * gs_ref[...] * (1.0 - lambda_init)
        o_ref[pl.ds(q0, tq), :] = a.T.astype(BF16)

    q_tile(0, False)

    def later_tile(i, c):
        q_tile(i, True)
        return c

    lax.fori_loop(1, nq, later_tile, 0)


def _diff_attn(lam_params, k, qt, vt, g_subln_col, lambda_init):
    B, H, S, _ = k.shape
    return pl.pallas_call(
        functools.partial(_diff_attn_kernel, lambda_init=lambda_init),
        grid=(B, H),
        in_specs=[
            _const_spec(lam_params.shape),
            pl.BlockSpec((None, None, S, LANES), lambda b, h: (b, h, 0, 0)),
            pl.BlockSpec((None, LANES, S), lambda b, h: (b, h, 0)),
            pl.BlockSpec((None, V_DIM, S), lambda b, h: (b, h, 0)),
            _const_spec(g_subln_col.shape),
        ],
        out_specs=pl.BlockSpec((None, None, S, V_DIM), lambda b, h: (b, h, 0, 0)),
        out_shape=jax.ShapeDtypeStruct((B, H, S, V_DIM), BF16),
        scratch_shapes=[
            pltpu.VMEM((LANES, 2 * ATT_Q), BF16),
            pltpu.VMEM((ATT_K, 2 * ATT_Q), F32),
            pltpu.VMEM((1, 2 * ATT_Q), F32),
            pltpu.VMEM((ATT_K, 2 * ATT_Q), BF16),
            pltpu.VMEM((1, 2 * ATT_Q), F32),
            pltpu.VMEM((1, 2 * ATT_Q), F32),
            pltpu.VMEM((V_DIM + BF16_ROWS, 2 * ATT_Q), F32),
        ],
        compiler_params=pltpu.CompilerParams(
            dimension_semantics=("arbitrary", "arbitrary"),
            vmem_limit_bytes=VMEM_LIMIT_BYTES),
        name="diff_attn",
    )(lam_params, k, qt, vt, g_subln_col)


def _out_mlp_kernel(x_ref, att_ref, gy_ref, g1_ref, wao_ref, wo_ref, gm_ref, wup_ref,
                    wdn_ref, gn_ref, o_ref, *, final_norm):
    att = jnp.concatenate([att_ref[h] for h in range(att_ref.shape[0])], axis=1)
    y_attn = jnp.dot(att, wao_ref[...], preferred_element_type=F32)
    merged = gy_ref[...].astype(F32) + g1_ref[...].astype(F32) * y_attn
    x1 = x_ref[...] + jnp.dot(merged.astype(BF16), wo_ref[...], preferred_element_type=F32)
    h2 = _rms(x1, gm_ref[...], NORM_EPS).astype(BF16)
    acc = x1
    d_ff = wup_ref.shape[1]
    for c in range(d_ff // FF_CHUNK):
        cols = slice(c * FF_CHUNK, (c + 1) * FF_CHUNK)
        up = jnp.dot(h2, wup_ref[:, cols], preferred_element_type=F32)
        act = jnp.square(jnp.maximum(up, 0.0)).astype(BF16)
        acc = acc + jnp.dot(act, wdn_ref[cols, :], preferred_element_type=F32)
    o_ref[...] = _rms(acc, gn_ref[...], NORM_EPS) if final_norm else acc


def _out_mlp(x, att, gy, g1, wao, wo, g_mlp, wup, wdn, g_final, final_norm):
    B, S, D = x.shape
    tm = MLP_TOKENS
    tok = lambda b, j: (b, j, 0)
    tile = pl.BlockSpec((None, tm, D), tok)
    return pl.pallas_call(
        functools.partial(_out_mlp_kernel, final_norm=final_norm),
        grid=(B, S // tm),
        in_specs=[tile,
                  pl.BlockSpec((None, att.shape[1], tm, att.shape[3]), lambda b, j: (b, 0, j, 0)),
                  tile, tile,
                  _const_spec(wao.shape), _const_spec(wo.shape), _const_spec(g_mlp.shape),
                  _const_spec(wup.shape), _const_spec(wdn.shape), _const_spec(g_final.shape)],
        out_specs=tile,
        out_shape=jax.ShapeDtypeStruct((B, S, D), F32),
        compiler_params=pltpu.CompilerParams(
            dimension_semantics=("arbitrary", "arbitrary"),
            vmem_limit_bytes=VMEM_LIMIT_BYTES),
        name="out_mlp",
    )(x, att, gy, g1, wao, wo, g_mlp, wup, wdn, g_final)


def _rope_tables(seq):
    pos = jnp.arange(seq, dtype=F32)
    inv = ROPE_THETA ** (-jnp.arange(0, ROPE_DIM, 2, dtype=F32) / ROPE_DIM)
    ang_t = inv[:, None] * pos[None, :]
    d = jnp.arange(LANES) % HEAD_DIM
    freq = jnp.where(d < ROPE_DIM, inv[d % ROPE_HALF], 0.0)
    ang = pos[:, None] * freq[None, :]
    cos, sin = jnp.cos(ang), jnp.sin(ang)
    sa = jnp.where(d < ROPE_HALF, -sin, 0.0)
    sb = jnp.where((d >= ROPE_HALF) & (d < ROPE_DIM), sin, 0.0)
    return cos, sa, sb, jnp.cos(ang_t), jnp.sin(ang_t)


def kernel(x, w_in, b_gate, pool_w, pool_scale, lambda_q1, lambda_k1, lambda_q2, lambda_k2,
           g_subln, w_pool_out, w_attn_out, w_o, g_mix, g_mlp, w_up, w_down, g_final):
    B, S, D = x.shape
    depth = w_in.shape[0]
    pool_width = pool_scale.shape[1]
    qk_width = (w_in.shape[2] - pool_width - 2 * D) // 3
    o1 = pool_width
    o2 = o1 + qk_width
    o3 = o2 + qk_width
    o4 = o3 + qk_width
    kc, ksa, ksb, qc, qs = _rope_tables(S)
    q_scale = HEAD_DIM ** -0.5 * LOG2E

    for l in range(depth):
        lambda_init = 0.8 - 0.6 * math.exp(-0.3 * l)
        w = w_in[l]
        wu = w[:, :o1].astype(BF16)
        wqt = w[:, o1:o2].T.astype(BF16)
        wk = w[:, o2:o3].astype(BF16)
        wvt = w[:, o3:o4].T.astype(BF16)
        wg = w[:, o4:].astype(BF16)
        k, qt, vt, gy, g1 = _in_proj(
            x, g_mix[l][None, :], wu, wk, wg, wqt, wvt, b_gate[l],
            pool_w[l].astype(BF16), pool_scale[l][None, :], w_pool_out[l].astype(BF16),
            kc, ksa, ksb, qc, qs, q_scale)
        lam_params = jnp.stack([lambda_q1[l], lambda_k1[l], lambda_q2[l], lambda_k2[l]])
        att = _diff_attn(lam_params.astype(F32), k, qt, vt,
                         g_subln[l].astype(F32)[:, None], lambda_init)
        x = _out_mlp(x, att, gy, g1, w_attn_out[l].astype(BF16), w_o[l].astype(BF16),
                     g_mlp[l][None, :], w_up[l].astype(BF16), w_down[l].astype(BF16),
                     g_final[None, :], final_norm=(l == depth - 1))
    return x
```
